```python
import math
import jax
import jax.numpy as jnp
from jax import lax
import numpy as np

D_MODEL = 2048
BATCH = 2
SEQ = 4096
DEPTH = 2

CHUNK = 64
D_RNN = D_MODEL
LRU_HEADS = 8
LRU_BLOCK = D_RNN // LRU_HEADS
LRU_C = 8.0
CONV_W = 4
D_POOL = D_MODEL // 2
POOL_WINDOWS = (2, 4, 8, 16)
POOL_GROUPS = len(POOL_WINDOWS)
POOL_GW = D_POOL // POOL_GROUPS
N_BRANCH = 2
D_IN = 2 * D_RNN + D_POOL + N_BRANCH * D_MODEL
D_FF = 5632
FFN_CONV_W = 3
N_ADA = 6
EPS = 1e-6

kernel_name = "hybrid_rglru_pool_streaming_trunk"


def rms_norm(x, g):
    xf = x.astype(jnp.float32)
    y = xf * lax.rsqrt(jnp.mean(xf * xf, axis=-1, keepdims=True) + EPS)
    return (y * g.astype(jnp.float32)).astype(x.dtype)


def causal_dwconv(x, w, b):
    k_w = w.shape[0]
    s = x.shape[1]
    xp = jnp.pad(x, ((0, 0), (k_w - 1, 0), (0, 0)))
    y = b
    for k in range(k_w):
        y = y + xp[:, k:k + s] * w[k]
    return y


def _lru_combine(left, right):
    a1, b1 = left
    a2, b2 = right
    return a1 * a2, a2 * b1 + b2


def rg_lru(x, wa, ba, wx, bx, lam):
    bsz, s, _ = x.shape
    xh = x.reshape(bsz, s, LRU_HEADS, LRU_BLOCK)
    r = jax.nn.sigmoid((jnp.einsum('bshi,hij->bshj', xh, wa).reshape(bsz, s, D_RNN) + ba).astype(jnp.float32))
    i = jax.nn.sigmoid((jnp.einsum('bshi,hij->bshj', xh, wx).reshape(bsz, s, D_RNN) + bx).astype(jnp.float32))
    log_a = LRU_C * r * jax.nn.log_sigmoid(lam.astype(jnp.float32))
    a = jnp.exp(log_a)
    inp = jnp.sqrt(-jnp.expm1(2.0 * log_a)) * (i * x.astype(jnp.float32))
    _, h = lax.associative_scan(_lru_combine, (a, inp), axis=1)
    return h.astype(x.dtype)


def multiscale_pool(x, w, b, scale):
    bsz, s, _ = x.shape
    xg = x.reshape(bsz, s, POOL_GROUPS, POOL_GW).astype(jnp.float32)
    pos = jnp.arange(s, dtype=jnp.float32)
    pooled = []
    for g, win in enumerate(POOL_WINDOWS):
        xi = xg[:, :, g]
        cs0 = jnp.pad(jnp.cumsum(xi, axis=1), ((0, 0), (1, 0), (0, 0)))
        upper = cs0[:, 1:]
        lower = jnp.pad(cs0[:, :s + 1 - win], ((0, 0), (win - 1, 0), (0, 0)))
        count = jnp.minimum(pos + 1.0, float(win))[None, :, None]
        pooled.append((upper - lower) / count - xi)
    p = jnp.stack(pooled, axis=2).astype(x.dtype)
    y = jnp.einsum('bsgc,gcd->bsgd', p, w).reshape(bsz, s, D_POOL) + b
    return y * scale


def setup_inputs(seed: int = 0) -> dict:
    key = jax.random.key(seed)
    ks = jax.random.split(key, 32)
    L, D = DEPTH, D_MODEL
    f32 = jnp.float32

    def nrm(k, shape, fan_in, mult=1.0):
        return jax.random.normal(k, shape, f32) * (mult * fan_in ** -0.5)

    def small(k, shape, s=0.02):
        return jax.random.normal(k, shape, f32) * s

    u = jax.random.uniform(ks[10], (L, D_RNN), f32, 0.9, 0.999)
    sig = u ** (1.0 / LRU_C)
    lam = jnp.log(sig) - jnp.log1p(-sig)

    return {
        "x": jax.random.normal(ks[0], (BATCH, SEQ, D), f32),
        "c": jax.random.normal(ks[1], (BATCH, D), f32),
        "ada_w": nrm(ks[2], (L, D, N_ADA * D), D, 0.5),
        "ada_b": small(ks[3], (L, N_ADA * D)),
        "norm_mix_g": 1.0 + small(ks[4], (L, D)),
        "w_in": nrm(ks[5], (L, D, D_IN), D),
        "b_in": small(ks[6], (L, D_IN)),
        "conv_w": nrm(ks[7], (L, CONV_W, D_RNN), CONV_W),
        "conv_b": small(ks[8], (L, D_RNN)),
        "lru_wa": nrm(ks[9], (L, LRU_HEADS, LRU_BLOCK, LRU_BLOCK), LRU_BLOCK),
        "lru_ba": small(ks[11], (L, D_RNN)),
        "lru_wx": nrm(ks[12], (L, LRU_HEADS, LRU_BLOCK, LRU_BLOCK), LRU_BLOCK),
        "lru_bx": small(ks[13], (L, D_RNN)),
        "lru_lambda": lam,
        "pool_w": nrm(ks[14], (L, POOL_GROUPS, POOL_GW, POOL_GW), POOL_GW),
        "pool_b": small(ks[15], (L, D_POOL)),
        "pool_scale": 1.0 + small(ks[16], (L, D_POOL), 0.1),
        "proj_a": nrm(ks[17], (L, D_RNN, D), D_RNN),
        "proj_b": nrm(ks[18], (L, D_POOL, D), D_POOL),
        "w_out": nrm(ks[19], (L, D, D), D),
        "norm_ffn_g": 1.0 + small(ks[20], (L, D)),
        "w_up": nrm(ks[21], (L, D, 2 * D_FF), D),
        "ffn_conv_w": nrm(ks[22], (L, FFN_CONV_W, 2 * D_FF), FFN_CONV_W),
        "ffn_conv_b": small(ks[23], (L, 2 * D_FF)),
        "w_down": nrm(ks[24], (L, D_FF, D), D_FF),
        "final_g": 1.0 + small(ks[25], (D,)),
    }


def reference(x, c, ada_w, ada_b, norm_mix_g, w_in, b_in, conv_w, conv_b,
              lru_wa, lru_ba, lru_wx, lru_bx, lru_lambda, pool_w, pool_b,
              pool_scale, proj_a, proj_b, w_out, norm_ffn_g, w_up, ffn_conv_w,
              ffn_conv_b, w_down, final_g):
    c_act = jax.nn.silu(c)
    for l in range(DEPTH):
        mod = c_act @ ada_w[l] + ada_b[l]
        sh1, sc1, gt1, sh2, sc2, gt2 = [m[:, None, :] for m in jnp.split(mod, N_ADA, axis=-1)]

        h = rms_norm(x, norm_mix_g[l]) * (1.0 + sc1) + sh1
        z = h @ w_in[l] + b_in[l]
        x_rnn, g_rnn, x_pool, g_br = jnp.split(
            z, [D_RNN, 2 * D_RNN, 2 * D_RNN + D_POOL], axis=-1)
        xr = causal_dwconv(x_rnn, conv_w[l], conv_b[l])
        ya = rg_lru(xr, lru_wa[l], lru_ba[l], lru_wx[l], lru_bx[l], lru_lambda[l])
        ya = ya * jax.nn.gelu(g_rnn)
        yb = multiscale_pool(x_pool, pool_w[l], pool_b[l], pool_scale[l])
        gate_a, gate_b = jnp.split(jax.nn.sigmoid(g_br), N_BRANCH, axis=-1)
        merged = gate_a * (ya @ proj_a[l]) + gate_b * (yb @ proj_b[l])
        x = x + gt1 * (merged @ w_out[l])

        h = rms_norm(x, norm_ffn_g[l]) * (1.0 + sc2) + sh2
        up = causal_dwconv(h @ w_up[l], ffn_conv_w[l], ffn_conv_b[l])
        u_act, u_lin = jnp.split(up, 2, axis=-1)
        x = x + gt2 * ((jax.nn.silu(u_act) * u_lin) @ w_down[l])

    return rms_norm(x, final_g)
```

```python
import functools

import jax
import jax.numpy as jnp
from jax import lax
from jax.experimental import pallas as pl
from jax.experimental.pallas import tpu as pltpu

F32 = jnp.float32
BF16 = jnp.bfloat16

LRU_HEADS = 8
LRU_C = 8.0
POOL_WINDOWS = (2, 4, 8, 16)
N_ADA = 6
EPS = 1e-6
GELU_C0 = 0.7978845608028654
GELU_C1 = 0.044715

SUBLANES = 8
POOL_HALO = 16
VMEM_LIMIT = 56 * 1024 * 1024


def _sigmoid(x):
    return 1.0 / (1.0 + jnp.exp(-x))


def _gelu_tanh(x):
    return 0.5 * x * (1.0 + jnp.tanh(GELU_C0 * (x + GELU_C1 * (x * x * x))))


def _log_sigmoid(x):
    return jnp.minimum(x, 0.0) - jnp.log1p(jnp.exp(-jnp.abs(x)))


def _one_minus_exp(z, u):
    one = u == 1.0
    return jnp.where(one, -z, z * (1.0 - u) / jnp.where(one, 1.0, jnp.log(u)))


def _norm_mod(x, g, scale, shift):
    ms = jnp.mean(x * x, axis=-1, keepdims=True)
    y = (x * lax.rsqrt(ms + EPS)) * g
    return y * (1.0 + scale) + shift


def _shift_rows(v, k):
    return pltpu.roll(v, k, 0)


def _ada_kernel(c_ref, w_ref, b_ref, o_ref):
    c = c_ref[...]
    ca = (c * _sigmoid(c)).astype(BF16)
    o_ref[0] = jnp.dot(ca, w_ref[0].astype(BF16), preferred_element_type=F32) + b_ref[0]


def _ada_call(c_pad, ada_w, ada_b):
    n_layers, d, n_out = ada_w.shape
    tn = 2048
    rows = c_pad.shape[0]
    return pl.pallas_call(
        _ada_kernel,
        grid=(n_layers, n_out // tn),
        in_specs=[
            pl.BlockSpec((rows, d), lambda l, n: (0, 0)),
            pl.BlockSpec((1, d, tn), lambda l, n: (l, 0, n)),
            pl.BlockSpec((1, 1, tn), lambda l, n: (l, 0, n)),
        ],
        out_specs=pl.BlockSpec((1, rows, tn), lambda l, n: (l, 0, n)),
        out_shape=jax.ShapeDtypeStruct((n_layers, rows, n_out), F32),
        compiler_params=pltpu.CompilerParams(
            dimension_semantics=("arbitrary", "arbitrary"), vmem_limit_bytes=VMEM_LIMIT),
        name="ada_mod",
    )(c_pad, ada_w, ada_b.reshape(n_layers, 1, n_out))


def _inproj_kernel(x_ref, mod_ref, g_ref, w_ref, b_ref, zx_ref, zg_ref, h_ref, *, n_x, n_gelu):
    n = pl.program_id(2)

    @pl.when(n == 0)
    def _():
        h_ref[...] = _norm_mod(x_ref[0], g_ref[0], mod_ref[0, 1:2, :], mod_ref[0, 0:1, :]).astype(BF16)

    z = jnp.dot(h_ref[...], w_ref[0].astype(BF16), preferred_element_type=F32) + b_ref[0]

    @pl.when(n < n_x)
    def _():
        zx_ref[0] = z

    @pl.when(jnp.logical_and(n >= n_x, n < n_x + n_gelu))
    def _():
        zg_ref[0] = _gelu_tanh(z).astype(BF16)

    @pl.when(n >= n_x + n_gelu)
    def _():
        zg_ref[0] = _sigmoid(z).astype(BF16)


def _inproj_call(x, mod, norm_g, w_in, b_in, layer, *, d_rnn, d_pool, ts, tn):
    bsz, seq, d = x.shape
    d_in = w_in.shape[-1]
    n_layers = w_in.shape[0]
    d_gates = d_in - 2 * d_rnn - d_pool
    nb_rnn, nb_pool, nb_gates = d_rnn // tn, d_pool // tn, d_gates // tn
    n_x = nb_rnn + nb_pool
    n_steps = n_x + nb_rnn + nb_gates

    def w_col(n):
        return jnp.where(n < nb_rnn, n,
                         jnp.where(n < n_x, n + nb_rnn,
                                   jnp.where(n < n_x + nb_rnn, n - nb_pool, n)))

    kern = functools.partial(_inproj_kernel, n_x=n_x, n_gelu=nb_rnn)
    return pl.pallas_call(
        kern,
        grid=(bsz, seq // ts, n_steps),
        in_specs=[
            pl.BlockSpec((1, ts, d), lambda b, t, n: (b, t, 0), pipeline_mode=pl.Buffered(1)),
            pl.BlockSpec((1, N_ADA, d), lambda b, t, n: (b, 0, 0)),
            pl.BlockSpec((1, 1, d), lambda b, t, n: (layer, 0, 0)),
            pl.BlockSpec((1, d, tn), lambda b, t, n: (layer, 0, w_col(n))),
            pl.BlockSpec((1, 1, tn), lambda b, t, n: (layer, 0, w_col(n))),
        ],
        out_specs=[
            pl.BlockSpec((1, ts, tn), lambda b, t, n: (b, t, jnp.minimum(n, n_x - 1))),
            pl.BlockSpec((1, ts, tn), lambda b, t, n: (b, t, jnp.maximum(n - n_x, 0))),
        ],
        out_shape=[
            jax.ShapeDtypeStruct((bsz, seq, d_rnn + d_pool), F32),
            jax.ShapeDtypeStruct((bsz, seq, d_rnn + d_gates), BF16),
        ],
        scratch_shapes=[pltpu.VMEM((ts, d), BF16)],
        compiler_params=pltpu.CompilerParams(
            dimension_semantics=("arbitrary", "arbitrary", "arbitrary"), vmem_limit_bytes=VMEM_LIMIT),
        name="in_proj",
    )(x, mod, norm_g.reshape(n_layers, 1, d), w_in, b_in.reshape(n_layers, 1, d_in))


def _rglru_kernel(xr_ref, gg_ref, cw_ref, cb_ref, wa_ref, ba_ref, wx_ref, bx_ref, lam_ref,
                  ya_ref, xbuf, a_buf, b_buf, h_buf, hcar, *, ts, conv_w):
    t = pl.program_id(2)
    halo = SUBLANES

    @pl.when(t == 0)
    def _():
        xbuf[0:halo, :] = jnp.zeros((halo, xbuf.shape[1]), F32)
        hcar[...] = jnp.zeros(hcar.shape, F32)

    xbuf[halo:halo + ts, :] = xr_ref[0]
    v = xbuf[...]
    cw = cw_ref[0]
    acc = cw[conv_w - 1:conv_w, :] * v
    for k in range(1, conv_w):
        acc = acc + cw[conv_w - 1 - k:conv_w - k, :] * _shift_rows(v, k)
    xr = acc[halo:, :] + cb_ref[0]
    xbuf[0:halo, :] = v[ts:ts + halo, :]

    xrb = xr.astype(BF16)
    r = _sigmoid(jnp.dot(xrb, wa_ref[0, 0].astype(BF16), preferred_element_type=F32) + ba_ref[0])
    i = _sigmoid(jnp.dot(xrb, wx_ref[0, 0].astype(BF16), preferred_element_type=F32) + bx_ref[0])
    log_a = (LRU_C * r) * _log_sigmoid(lam_ref[0])
    a = jnp.exp(log_a)
    b = jnp.sqrt(_one_minus_exp(2.0 * log_a, a * a)) * (i * xr)

    rowm = lax.broadcasted_iota(jnp.int32, a.shape, 0) & (SUBLANES - 1)
    s = 1
    while s < SUBLANES:
        keep = rowm >= s
        a_sh = jnp.where(keep, _shift_rows(a, s), 1.0)
        b_sh = jnp.where(keep, _shift_rows(b, s), 0.0)
        b = a * b_sh + b
        a = a * a_sh
        s *= 2
    a_buf[...] = a
    b_buf[...] = b

    def body(g, hc):
        r0 = pl.multiple_of(g * SUBLANES, SUBLANES)
        h8 = a_buf[pl.ds(r0, SUBLANES), :] * hc + b_buf[pl.ds(r0, SUBLANES), :]
        h_buf[pl.ds(r0, SUBLANES), :] = h8
        return jnp.broadcast_to(h8[SUBLANES - 1:SUBLANES, :], hc.shape)

    hcar[...] = lax.fori_loop(0, ts // SUBLANES, body, hcar[...], unroll=8)
    ya_ref[0] = (h_buf[...] * gg_ref[0].astype(F32)).astype(BF16)


def _rglru_call(zx, zg, conv_w, conv_b, lru_wa, lru_ba, lru_wx, lru_bx, lru_lambda, layer, *, d_rnn, ts):
    bsz, seq, _ = zx.shape
    n_layers, kw, _ = conv_w.shape
    heads, wblk = lru_wa.shape[1], lru_wa.shape[2]
    vec = lambda a: a.reshape(n_layers, 1, d_rnn)
    vspec = pl.BlockSpec((1, 1, wblk), lambda b, j, t: (layer, 0, j))
    mspec = pl.BlockSpec((1, 1, wblk, wblk), lambda b, j, t: (layer, j, 0, 0))
    kern = functools.partial(_rglru_kernel, ts=ts, conv_w=kw)
    return pl.pallas_call(
        kern,
        grid=(bsz, heads, seq // ts),
        in_specs=[
            pl.BlockSpec((1, ts, wblk), lambda b, j, t: (b, t, j)),
            pl.BlockSpec((1, ts, wblk), lambda b, j, t: (b, t, j)),
            pl.BlockSpec((1, kw, wblk), lambda b, j, t: (layer, 0, j)),
            vspec, mspec, vspec, mspec, vspec, vspec,
        ],
        out_specs=pl.BlockSpec((1, ts, wblk), lambda b, j, t: (b, t, j)),
        out_shape=jax.ShapeDtypeStruct((bsz, seq, d_rnn), BF16),
        scratch_shapes=[
            pltpu.VMEM((SUBLANES + ts, wblk), F32),
            pltpu.VMEM((ts, wblk), F32),
            pltpu.VMEM((ts, wblk), F32),
            pltpu.VMEM((ts, wblk), F32),
            pltpu.VMEM((SUBLANES, wblk), F32),
        ],
        compiler_params=pltpu.CompilerParams(
            dimension_semantics=("arbitrary", "arbitrary", "arbitrary"), vmem_limit_bytes=VMEM_LIMIT),
        name="rg_lru",
    )(zx, zg, conv_w, vec(conv_b), lru_wa, vec(lru_ba), lru_wx, vec(lru_bx), vec(lru_lambda))


def _pool_kernel(xp_ref, w_ref, b_ref, s_ref, yb_ref, xbuf, *, ts, gw):
    t = pl.program_id(1)
    halo = POOL_HALO

    @pl.when(t == 0)
    def _():
        xbuf[0:halo, :] = jnp.zeros((halo, xbuf.shape[1]), F32)

    xbuf[halo:halo + ts, :] = xp_ref[0]
    pos = (t * ts + lax.broadcasted_iota(jnp.int32, (ts, gw), 0)).astype(F32)
    for g, win in enumerate(POOL_WINDOWS):
        v = xbuf[:, g * gw:(g + 1) * gw]
        sm = v
        k = 1
        while k < win:
            sm = sm + _shift_rows(sm, k)
            k *= 2
        cur = v[halo:, :]
        count = jnp.minimum(pos + 1.0, float(win))
        p = (sm[halo:, :] / count - cur).astype(BF16)
        y = jnp.dot(p, w_ref[0, g].astype(BF16), preferred_element_type=F32) + b_ref[0, :, g * gw:(g + 1) * gw]
        yb_ref[0, :, g * gw:(g + 1) * gw] = (y * s_ref[0, :, g * gw:(g + 1) * gw]).astype(BF16)
    xbuf[0:halo, :] = xbuf[ts:ts + halo, :]


def _pool_call(zx, pool_w, pool_b, pool_scale, layer, *, d_rnn, d_pool, ts):
    bsz, seq, _ = zx.shape
    n_layers, groups, gw, _ = pool_w.shape
    col0 = d_rnn // d_pool
    vec = lambda a: a.reshape(n_layers, 1, d_pool)
    vspec = pl.BlockSpec((1, 1, d_pool), lambda b, t: (layer, 0, 0))
    kern = functools.partial(_pool_kernel, ts=ts, gw=gw)
    return pl.pallas_call(
        kern,
        grid=(bsz, seq // ts),
        in_specs=[
            pl.BlockSpec((1, ts, d_pool), lambda b, t: (b, t, col0)),
            pl.BlockSpec((1, groups, gw, gw), lambda b, t: (layer, 0, 0, 0)),
            vspec, vspec,
        ],
        out_specs=pl.BlockSpec((1, ts, d_pool), lambda b, t: (b, t, 0)),
        out_shape=jax.ShapeDtypeStruct((bsz, seq, d_pool), BF16),
        scratch_shapes=[pltpu.VMEM((POOL_HALO + ts, d_pool), F32)],
        compiler_params=pltpu.CompilerParams(
            dimension_semantics=("arbitrary", "arbitrary"), vmem_limit_bytes=VMEM_LIMIT),
        name="ms_pool",
    )(zx, pool_w, vec(pool_b), vec(pool_scale))


def _merge_kernel(ya_ref, yb_ref, ga_ref, gb_ref, pa_ref, pb_ref, wo_ref, x_ref, mod_ref,
                  o_ref, m_ref, *, nb, bn):
    n = pl.program_id(2)

    @pl.when(n < nb)
    def _():
        pa = jnp.dot(ya_ref[0], pa_ref[0].astype(BF16), preferred_element_type=F32)
        pb = jnp.dot(yb_ref[0], pb_ref[0].astype(BF16), preferred_element_type=F32)
        merged = ga_ref[0].astype(F32) * pa + gb_ref[0].astype(F32) * pb
        c0 = pl.multiple_of(n * bn, bn)
        m_ref[:, pl.ds(c0, bn)] = merged.astype(BF16)

    @pl.when(n >= nb)
    def _():
        y = jnp.dot(m_ref[...], wo_ref[0].astype(BF16), preferred_element_type=F32)
        o_ref[0] = x_ref[0] + mod_ref[0, 2:3, :] * y


def _merge_call(ya, yb, zg, proj_a, proj_b, w_out, x, mod, layer, *, ts, bn):
    bsz, seq, d = x.shape
    d_rnn, d_pool = ya.shape[-1], yb.shape[-1]
    nb = d // bn
    ga0 = d_rnn // bn
    gb0 = ga0 + nb
    first = lambda n: jnp.minimum(n, nb - 1)
    second = lambda n: jnp.maximum(n - nb, 0)
    kern = functools.partial(_merge_kernel, nb=nb, bn=bn)
    return pl.pallas_call(
        kern,
        grid=(bsz, seq // ts, 2 * nb),
        in_specs=[
            pl.BlockSpec((1, ts, d_rnn), lambda b, t, n: (b, t, 0)),
            pl.BlockSpec((1, ts, d_pool), lambda b, t, n: (b, t, 0)),
            pl.BlockSpec((1, ts, bn), lambda b, t, n: (b, t, ga0 + first(n))),
            pl.BlockSpec((1, ts, bn), lambda b, t, n: (b, t, gb0 + first(n))),
            pl.BlockSpec((1, d_rnn, bn), lambda b, t, n: (layer, 0, first(n))),
            pl.BlockSpec((1, d_pool, bn), lambda b, t, n: (layer, 0, first(n))),
            pl.BlockSpec((1, d, bn), lambda b, t, n: (layer, 0, second(n))),
            pl.BlockSpec((1, ts, bn), lambda b, t, n: (b, t, second(n))),
            pl.BlockSpec((1, N_ADA, bn), lambda b, t, n: (b, 0, second(n))),
        ],
        out_specs=pl.BlockSpec((1, ts, bn), lambda b, t, n: (b, t, second(n))),
        out_shape=jax.ShapeDtypeStruct((bsz, seq, d), F32),
        scratch_shapes=[pltpu.VMEM((ts, d), BF16)],
        compiler_params=pltpu.CompilerParams(
            dimension_semantics=("arbitrary", "arbitrary", "arbitrary"), vmem_limit_bytes=VMEM_LIMIT),
        name="merge",
    )(ya, yb, zg, zg, proj_a, proj_b, w_out, x, mod)


def _ffn_kernel(x_ref, mod_ref, g_ref, wua_ref, wul_ref, cwa_ref, cwl_ref, cba_ref, cbl_ref,
                wd_ref, fg_ref, o_ref, h_ref, car_a, car_l, *, ts, nf, conv_w, final):
    t = pl.program_id(1)
    f = pl.program_id(2)
    halo = SUBLANES

    @pl.when(f == 0)
    def _():
        h_ref[...] = _norm_mod(x_ref[0], g_ref[0], mod_ref[0, 4:5, :], mod_ref[0, 3:4, :]).astype(BF16)
        o_ref[0] = jnp.zeros(o_ref.shape[1:], F32)

    @pl.when(t == 0)
    def _():
        car_a[f] = jnp.zeros(car_a.shape[1:], F32)
        car_l[f] = jnp.zeros(car_l.shape[1:], F32)

    def conv(u, car, cw_ref, cb_ref):
        v = jnp.concatenate([car[f], u], axis=0)
        cw = cw_ref[0]
        acc = cw[conv_w - 1:conv_w, :] * v
        for k in range(1, conv_w):
            acc = acc + cw[conv_w - 1 - k:conv_w - k, :] * _shift_rows(v, k)
        car[f] = u[ts - halo:, :]
        return acc[halo:, :] + cb_ref[0]

    h = h_ref[...]
    ua = jnp.dot(h, wua_ref[0].astype(BF16), preferred_element_type=F32)
    ul = jnp.dot(h, wul_ref[0].astype(BF16), preferred_element_type=F32)
    ca = conv(ua, car_a, cwa_ref, cba_ref)
    cl = conv(ul, car_l, cwl_ref, cbl_ref)
    p = ((ca * _sigmoid(ca)) * cl).astype(BF16)
    o_ref[0] += jnp.dot(p, wd_ref[0].astype(BF16), preferred_element_type=F32)

    @pl.when(f == nf - 1)
    def _():
        y = x_ref[0] + mod_ref[0, 5:6, :] * o_ref[0]
        if final:
            ms = jnp.mean(y * y, axis=-1, keepdims=True)
            y = (y * lax.rsqrt(ms + EPS)) * fg_ref[...]
        o_ref[0] = y


def _ffn_call(x, mod, norm_g, w_up, ffn_conv_w, ffn_conv_b, w_down, final_g, layer, *, ts, tf, final):
    bsz, seq, d = x.shape
    n_layers, d_ff, _ = w_down.shape
    kw = ffn_conv_w.shape[1]
    nf = d_ff // tf
    cb = ffn_conv_b.reshape(n_layers, 1, 2 * d_ff)
    kern = functools.partial(_ffn_kernel, ts=ts, nf=nf, conv_w=kw, final=final)
    return pl.pallas_call(
        kern,
        grid=(bsz, seq // ts, nf),
        in_specs=[
            pl.BlockSpec((1, ts, d), lambda b, t, f: (b, t, 0), pipeline_mode=pl.Buffered(1)),
            pl.BlockSpec((1, N_ADA, d), lambda b, t, f: (b, 0, 0)),
            pl.BlockSpec((1, 1, d), lambda b, t, f: (layer, 0, 0)),
            pl.BlockSpec((1, d, tf), lambda b, t, f: (layer, 0, f)),
            pl.BlockSpec((1, d, tf), lambda b, t, f: (layer, 0, nf + f)),
            pl.BlockSpec((1, kw, tf), lambda b, t, f: (layer, 0, f)),
            pl.BlockSpec((1, kw, tf), lambda b, t, f: (layer, 0, nf + f)),
            pl.BlockSpec((1, 1, tf), lambda b, t, f: (layer, 0, f)),
            pl.BlockSpec((1, 1, tf), lambda b, t, f: (layer, 0, nf + f)),
            pl.BlockSpec((1, tf, d), lambda b, t, f: (layer, f, 0)),
            pl.BlockSpec((1, d), lambda b, t, f: (0, 0)),
        ],
        out_specs=pl.BlockSpec((1, ts, d), lambda b, t, f: (b, t, 0)),
        out_shape=jax.ShapeDtypeStruct((bsz, seq, d), F32),
        scratch_shapes=[
            pltpu.VMEM((ts, d), BF16),
            pltpu.VMEM((nf, SUBLANES, tf), F32),
            pltpu.VMEM((nf, SUBLANES, tf), F32),
        ],
        compiler_params=pltpu.CompilerParams(
            dimension_semantics=("arbitrary", "arbitrary", "arbitrary"), vmem_limit_bytes=VMEM_LIMIT),
        name="conv_ffn",
    )(x, mod, norm_g.reshape(n_layers, 1, d), w_up, w_up, ffn_conv_w, ffn_conv_w, cb, cb,
      w_down, final_g.reshape(1, d))


def kernel(x, c, ada_w, ada_b, norm_mix_g, w_in, b_in, conv_w, conv_b, lru_wa, lru_ba, lru_wx, lru_bx, lru_lambda, pool_w, pool_b, pool_scale, proj_a, proj_b, w_out, norm_ffn_g, w_up, ffn_conv_w, ffn_conv_b, w_down, final_g):
    bsz, seq, d = x.shape
    n_layers = ada_w.shape[0]
    d_rnn = conv_w.shape[-1]
    d_pool = pool_b.shape[-1]
    assert lru_wa.shape[1] == LRU_HEADS and pool_w.shape[1] == len(POOL_WINDOWS)
    assert ada_w.shape[-1] == N_ADA * d

    c_pad = jnp.pad(c, ((0, SUBLANES - bsz), (0, 0)))
    mod_all = _ada_call(c_pad, ada_w, ada_b)[:, :bsz].reshape(n_layers, bsz, N_ADA, d)

    ts = 1024
    for l in range(n_layers):
        mod = mod_all[l]
        zx, zg = _inproj_call(x, mod, norm_mix_g, w_in, b_in, l, d_rnn=d_rnn, d_pool=d_pool, ts=ts, tn=1024)
        ya = _rglru_call(zx, zg, conv_w, conv_b, lru_wa, lru_ba, lru_wx, lru_bx, lru_lambda, l,
                         d_rnn=d_rnn, ts=ts)
        yb = _pool_call(zx, pool_w, pool_b, pool_scale, l, d_rnn=d_rnn, d_pool=d_pool, ts=ts)
        x = _merge_call(ya, yb, zg, proj_a, proj_b, w_out, x, mod, l, ts=ts, bn=256)
        x = _ffn_call(x, mod, norm_ffn_g, w_up, ffn_conv_w, ffn_conv_b, w_down, final_g, l,
                      ts=ts, tf=256, final=(l == n_layers - 1))
    return x
```

```python
import functools

import jax
import jax.numpy as jnp
from jax import lax
from jax.experimental import pallas as pl
from jax.experimental.pallas import tpu as pltpu

F32 = jnp.float32
BF16 = jnp.bfloat16

LRU_HEADS = 8
LRU_C = 8.0
POOL_WINDOWS = (2, 4, 8, 16)
N_ADA = 6
EPS = 1e-6
GELU_C0 = 0.7978845608028654
GELU_C1 = 0.044715

SUBLANES = 8
POOL_HALO = 16
VMEM_LIMIT = 56 * 1024 * 1024


def _sigmoid(x):
    return 1.0 / (1.0 + jnp.exp(-x))


def _gelu_tanh(x):
    return 0.5 * x * (1.0 + jnp.tanh(GELU_C0 * (x + GELU_C1 * (x * x * x))))


def _log_sigmoid(x):
    return jnp.minimum(x, 0.0) - jnp.log1p(jnp.exp(-jnp.abs(x)))


def _one_minus_exp(z, u):
    one = u == 1.0
    return jnp.where(one, -z, z * (1.0 - u) / jnp.where(one, 1.0, jnp.log(u)))


def _norm_mod(x, g, scale, shift):
    ms = jnp.mean(x * x, axis=-1, keepdims=True)
    y = (x * lax.rsqrt(ms + EPS)) * g
    return y * (1.0 + scale) + shift


def _shift_rows(v, k):
    return pltpu.roll(v, k, 0)


def _ada_kernel(c_ref, w_ref, b_ref, o_ref):
    c = c_ref[...]
    ca = (c * _sigmoid(c)).astype(BF16)
    o_ref[0] = jnp.dot(ca, w_ref[0].astype(BF16), preferred_element_type=F32) + b_ref[0]


def _ada_call(c_pad, ada_w, ada_b):
    n_layers, d, n_out = ada_w.shape
    tn = 2048
    rows = c_pad.shape[0]
    return pl.pallas_call(
        _ada_kernel,
        grid=(n_layers, n_out // tn),
        in_specs=[
            pl.BlockSpec((rows, d), lambda l, n: (0, 0)),
            pl.BlockSpec((1, d, tn), lambda l, n: (l, 0, n)),
            pl.BlockSpec((1, 1, tn), lambda l, n: (l, 0, n)),
        ],
        out_specs=pl.BlockSpec((1, rows, tn), lambda l, n: (l, 0, n)),
        out_shape=jax.ShapeDtypeStruct((n_layers, rows, n_out), F32),
        compiler_params=pltpu.CompilerParams(
            dimension_semantics=("arbitrary", "arbitrary"), vmem_limit_bytes=VMEM_LIMIT),
        name="ada_mod",
    )(c_pad, ada_w, ada_b.reshape(n_layers, 1, n_out))


def _inproj_kernel(x_ref, mod_ref, g_ref, w_ref, b_ref, zx_ref, zg_ref, h_ref, *, n_x, n_gelu):
    n = pl.program_id(2)

    @pl.when(n == 0)
    def _():
        h_ref[...] = _norm_mod(x_ref[0], g_ref[0], mod_ref[0, 1:2, :], mod_ref[0, 0:1, :]).astype(BF16)

    def z_block():
        return jnp.dot(h_ref[...], w_ref[0].astype(BF16), preferred_element_type=F32) + b_ref[0]

    @pl.when(n < n_x)
    def _():
        zx_ref[0] = z_block()

    @pl.when(n >= n_x)
    def _():
        z = z_block()
        is_gelu = n < n_x + n_gelu
        arg = jnp.where(is_gelu, GELU_C0 * (z + GELU_C1 * (z * z * z)), 0.5 * z)
        lead = jnp.where(is_gelu, 0.5 * z, 0.5)
        zg_ref[0] = (lead * (1.0 + jnp.tanh(arg))).astype(BF16)


def _inproj_call(x, mod, norm_g, w_in, b_in, layer, *, d_rnn, d_pool, ts, tn):
    bsz, seq, d = x.shape
    d_in = w_in.shape[-1]
    n_layers = w_in.shape[0]
    d_gates = d_in - 2 * d_rnn - d_pool
    nb_rnn, nb_pool, nb_gates = d_rnn // tn, d_pool // tn, d_gates // tn
    n_x = nb_rnn + nb_pool
    n_steps = n_x + nb_rnn + nb_gates

    def w_col(n):
        return jnp.where(n < nb_rnn, n,
                         jnp.where(n < n_x, n + nb_rnn,
                                   jnp.where(n < n_x + nb_rnn, n - nb_pool, n)))

    kern = functools.partial(_inproj_kernel, n_x=n_x, n_gelu=nb_rnn)
    return pl.pallas_call(
        kern,
        grid=(bsz, seq // ts, n_steps),
        in_specs=[
            pl.BlockSpec((1, ts, d), lambda b, t, n: (b, t, 0), pipeline_mode=pl.Buffered(1)),
            pl.BlockSpec((1, N_ADA, d), lambda b, t, n: (b, 0, 0)),
            pl.BlockSpec((1, 1, d), lambda b, t, n: (layer, 0, 0)),
            pl.BlockSpec((1, d, tn), lambda b, t, n: (layer, 0, w_col(n))),
            pl.BlockSpec((1, 1, tn), lambda b, t, n: (layer, 0, w_col(n))),
        ],
        out_specs=[
            pl.BlockSpec((1, ts, tn), lambda b, t, n: (b, t, jnp.minimum(n, n_x - 1))),
            pl.BlockSpec((1, ts, tn), lambda b, t, n: (b, t, jnp.maximum(n - n_x, 0))),
        ],
        out_shape=[
            jax.ShapeDtypeStruct((bsz, seq, d_rnn + d_pool), F32),
            jax.ShapeDtypeStruct((bsz, seq, d_rnn + d_gates), BF16),
        ],
        scratch_shapes=[pltpu.VMEM((ts, d), BF16)],
        compiler_params=pltpu.CompilerParams(
            dimension_semantics=("arbitrary", "arbitrary", "arbitrary"), vmem_limit_bytes=VMEM_LIMIT),
        name="in_proj",
    )(x, mod, norm_g.reshape(n_layers, 1, d), w_in, b_in.reshape(n_layers, 1, d_in))


def _rglru_kernel(xr_ref, gg_ref, cw_ref, cb_ref, wa_ref, ba_ref, wx_ref, bx_ref, lam_ref,
                  ya_ref, xbuf, a_buf, b_buf, h_buf, hcar, *, ts, conv_w):
    t = pl.program_id(2)
    halo = SUBLANES

    @pl.when(t == 0)
    def _():
        xbuf[0:halo, :] = jnp.zeros((halo, xbuf.shape[1]), F32)
        hcar[...] = jnp.zeros(hcar.shape, F32)

    xbuf[halo:halo + ts, :] = xr_ref[0]
    v = xbuf[...]
    cw = cw_ref[0]
    acc = cw[conv_w - 1:conv_w, :] * v
    for k in range(1, conv_w):
        acc = acc + cw[conv_w - 1 - k:conv_w - k, :] * _shift_rows(v, k)
    xr = acc[halo:, :] + cb_ref[0]
    xbuf[0:halo, :] = v[ts:ts + halo, :]

    xrb = xr.astype(BF16)
    r = _sigmoid(jnp.dot(xrb, wa_ref[0, 0].astype(BF16), preferred_element_type=F32) + ba_ref[0])
    i = _sigmoid(jnp.dot(xrb, wx_ref[0, 0].astype(BF16), preferred_element_type=F32) + bx_ref[0])
    log_a = (LRU_C * r) * _log_sigmoid(lam_ref[0])
    a = jnp.exp(log_a)
    b = jnp.sqrt(_one_minus_exp(2.0 * log_a, a * a)) * (i * xr)

    rowm = lax.broadcasted_iota(jnp.int32, a.shape, 0) & (SUBLANES - 1)
    s = 1
    while s < SUBLANES:
        keep = rowm >= s
        a_sh = jnp.where(keep, _shift_rows(a, s), 1.0)
        b_sh = jnp.where(keep, _shift_rows(b, s), 0.0)
        b = a * b_sh + b
        a = a * a_sh
        s *= 2
    a_buf[...] = a
    b_buf[...] = b

    def body(g, hc):
        r0 = pl.multiple_of(g * SUBLANES, SUBLANES)
        h8 = a_buf[pl.ds(r0, SUBLANES), :] * hc + b_buf[pl.ds(r0, SUBLANES), :]
        h_buf[pl.ds(r0, SUBLANES), :] = h8
        return jnp.broadcast_to(h8[SUBLANES - 1:SUBLANES, :], hc.shape)

    hcar[...] = lax.fori_loop(0, ts // SUBLANES, body, hcar[...], unroll=8)
    ya_ref[0] = (h_buf[...] * gg_ref[0].astype(F32)).astype(BF16)


def _rglru_call(zx, zg, conv_w, conv_b, lru_wa, lru_ba, lru_wx, lru_bx, lru_lambda, layer, *, d_rnn, ts):
    bsz, seq, _ = zx.shape
    n_layers, kw, _ = conv_w.shape
    heads, wblk = lru_wa.shape[1], lru_wa.shape[2]
    vec = lambda a: a.reshape(n_layers, 1, d_rnn)
    vspec = pl.BlockSpec((1, 1, wblk), lambda b, j, t: (layer, 0, j))
    mspec = pl.BlockSpec((1, 1, wblk, wblk), lambda b, j, t: (layer, j, 0, 0))
    kern = functools.partial(_rglru_kernel, ts=ts, conv_w=kw)
    return pl.pallas_call(
        kern,
        grid=(bsz, heads, seq // ts),
        in_specs=[
            pl.BlockSpec((1, ts, wblk), lambda b, j, t: (b, t, j)),
            pl.BlockSpec((1, ts, wblk), lambda b, j, t: (b, t, j)),
            pl.BlockSpec((1, kw, wblk), lambda b, j, t: (layer, 0, j)),
            vspec, mspec, vspec, mspec, vspec, vspec,
        ],
        out_specs=pl.BlockSpec((1, ts, wblk), lambda b, j, t: (b, t, j)),
        out_shape=jax.ShapeDtypeStruct((bsz, seq, d_rnn), BF16),
        scratch_shapes=[
            pltpu.VMEM((SUBLANES + ts, wblk), F32),
            pltpu.VMEM((ts, wblk), F32),
            pltpu.VMEM((ts, wblk), F32),
            pltpu.VMEM((ts, wblk), F32),
            pltpu.VMEM((SUBLANES, wblk), F32),
        ],
        compiler_params=pltpu.CompilerParams(
            dimension_semantics=("arbitrary", "arbitrary", "arbitrary"), vmem_limit_bytes=VMEM_LIMIT),
        name="rg_lru",
    )(zx, zg, conv_w, vec(conv_b), lru_wa, vec(lru_ba), lru_wx, vec(lru_bx), vec(lru_lambda))


def _pool_kernel(xp_ref, w_ref, b_ref, s_ref, yb_ref, xbuf, *, ts, gw):
    t = pl.program_id(1)
    halo = POOL_HALO

    @pl.when(t == 0)
    def _():
        xbuf[0:halo, :] = jnp.zeros((halo, xbuf.shape[1]), F32)

    xbuf[halo:halo + ts, :] = xp_ref[0]
    pos = (t * ts + lax.broadcasted_iota(jnp.int32, (ts, gw), 0)).astype(F32)
    for g, win in enumerate(POOL_WINDOWS):
        v = xbuf[:, g * gw:(g + 1) * gw]
        sm = v
        k = 1
        while k < win:
            sm = sm + _shift_rows(sm, k)
            k *= 2
        cur = v[halo:, :]
        count = jnp.minimum(pos + 1.0, float(win))
        p = (sm[halo:, :] / count - cur).astype(BF16)
        y = jnp.dot(p, w_ref[0, g].astype(BF16), preferred_element_type=F32) + b_ref[0, :, g * gw:(g + 1) * gw]
        yb_ref[0, :, g * gw:(g + 1) * gw] = (y * s_ref[0, :, g * gw:(g + 1) * gw]).astype(BF16)
    xbuf[0:halo, :] = xbuf[ts:ts + halo, :]


def _pool_call(zx, pool_w, pool_b, pool_scale, layer, *, d_rnn, d_pool, ts):
    bsz, seq, _ = zx.shape
    n_layers, groups, gw, _ = pool_w.shape
    col0 = d_rnn // d_pool
    vec = lambda a: a.reshape(n_layers, 1, d_pool)
    vspec = pl.BlockSpec((1, 1, d_pool), lambda b, t: (layer, 0, 0))
    kern = functools.partial(_pool_kernel, ts=ts, gw=gw)
    return pl.pallas_call(
        kern,
        grid=(bsz, seq // ts),
        in_specs=[
            pl.BlockSpec((1, ts, d_pool), lambda b, t: (b, t, col0)),
            pl.BlockSpec((1, groups, gw, gw), lambda b, t: (layer, 0, 0, 0)),
            vspec, vspec,
        ],
        out_specs=pl.BlockSpec((1, ts, d_pool), lambda b, t: (b, t, 0)),
        out_shape=jax.ShapeDtypeStruct((bsz, seq, d_pool), BF16),
        scratch_shapes=[pltpu.VMEM((POOL_HALO + ts, d_pool), F32)],
        compiler_params=pltpu.CompilerParams(
            dimension_semantics=("arbitrary", "arbitrary"), vmem_limit_bytes=VMEM_LIMIT),
        name="ms_pool",
    )(zx, pool_w, vec(pool_b), vec(pool_scale))


def _merge_kernel(ya_ref, yb_ref, ga_ref, gb_ref, pa_ref, pb_ref, wo_ref, x_ref, mod_ref,
                  o_ref, m_ref, *, nb, bn):
    n = pl.program_id(2)

    @pl.when(n < nb)
    def _():
        pa = jnp.dot(ya_ref[0], pa_ref[0].astype(BF16), preferred_element_type=F32)
        pb = jnp.dot(yb_ref[0], pb_ref[0].astype(BF16), preferred_element_type=F32)
        merged = ga_ref[0].astype(F32) * pa + gb_ref[0].astype(F32) * pb
        c0 = pl.multiple_of(n * bn, bn)
        m_ref[:, pl.ds(c0, bn)] = merged.astype(BF16)

    @pl.when(n >= nb)
    def _():
        y = jnp.dot(m_ref[...], wo_ref[0].astype(BF16), preferred_element_type=F32)
        o_ref[0] = x_ref[0] + mod_ref[0, 2:3, :] * y


def _merge_call(ya, yb, zg, proj_a, proj_b, w_out, x, mod, layer, *, ts, bn):
    bsz, seq, d = x.shape
    d_rnn, d_pool = ya.shape[-1], yb.shape[-1]
    nb = d // bn
    ga0 = d_rnn // bn
    gb0 = ga0 + nb
    first = lambda n: jnp.minimum(n, nb - 1)
    second = lambda n: jnp.maximum(n - nb, 0)
    kern = functools.partial(_merge_kernel, nb=nb, bn=bn)
    return pl.pallas_call(
        kern,
        grid=(bsz, seq // ts, 2 * nb),
        in_specs=[
            pl.BlockSpec((1, ts, d_rnn), lambda b, t, n: (b, t, 0)),
            pl.BlockSpec((1, ts, d_pool), lambda b, t, n: (b, t, 0)),
            pl.BlockSpec((1, ts, bn), lambda b, t, n: (b, t, ga0 + first(n))),
            pl.BlockSpec((1, ts, bn), lambda b, t, n: (b, t, gb0 + first(n))),
            pl.BlockSpec((1, d_rnn, bn), lambda b, t, n: (layer, 0, first(n))),
            pl.BlockSpec((1, d_pool, bn), lambda b, t, n: (layer, 0, first(n))),
            pl.BlockSpec((1, d, bn), lambda b, t, n: (layer, 0, second(n))),
            pl.BlockSpec((1, ts, bn), lambda b, t, n: (b, t, second(n))),
            pl.BlockSpec((1, N_ADA, bn), lambda b, t, n: (b, 0, second(n))),
        ],
        out_specs=pl.BlockSpec((1, ts, bn), lambda b, t, n: (b, t, second(n))),
        out_shape=jax.ShapeDtypeStruct((bsz, seq, d), F32),
        scratch_shapes=[pltpu.VMEM((ts, d), BF16)],
        compiler_params=pltpu.CompilerParams(
            dimension_semantics=("arbitrary", "arbitrary", "arbitrary"), vmem_limit_bytes=VMEM_LIMIT),
        name="merge",
    )(ya, yb, zg, zg, proj_a, proj_b, w_out, x, mod)


def _ffn_kernel(x_ref, mod_ref, g_ref, wua_ref, wul_ref, cwa_ref, cwl_ref, cba_ref, cbl_ref,
                wd_ref, fg_ref, o_ref, h_ref, ua_buf, ul_buf, car_a, car_l, *, ts, rc, nf, conv_w, final):
    t = pl.program_id(1)
    f = pl.program_id(2)
    halo = SUBLANES
    n_chunks = ts // rc

    def conv(v, cw_ref, cb_ref):
        cw = cw_ref[0]
        acc = cw[conv_w - 1:conv_w, :] * v
        for k in range(1, conv_w):
            acc = acc + cw[conv_w - 1 - k:conv_w - k, :] * _shift_rows(v, k)
        return acc[halo:, :] + cb_ref[0]

    def run(do_up, do_down, first):
        if do_up:
            wa = wua_ref[0].astype(BF16)
            wl = wul_ref[0].astype(BF16)
        if do_down:
            wd = wd_ref[0].astype(BF16)
            tail_a = car_a[f - 1]
            tail_l = car_l[f - 1]
        for c in range(n_chunks):
            rows = slice(c * rc, (c + 1) * rc)
            if do_down:
                ua = ua_buf[rows, :]
                ul = ul_buf[rows, :]
                ca = conv(jnp.concatenate([tail_a, ua], axis=0), cwa_ref, cba_ref)
                cl = conv(jnp.concatenate([tail_l, ul], axis=0), cwl_ref, cbl_ref)
                p = ((ca * _sigmoid(ca)) * cl).astype(BF16)
                tail_a = ua[rc - halo:, :]
                tail_l = ul[rc - halo:, :]
            if do_up:
                h = h_ref[rows, :]
                ua_buf[rows, :] = jnp.dot(h, wa, preferred_element_type=F32)
                ul_buf[rows, :] = jnp.dot(h, wl, preferred_element_type=F32)
            if do_down:
                y = jnp.dot(p, wd, preferred_element_type=F32)
                if first:
                    o_ref[0, rows, :] = y
                else:
                    o_ref[0, rows, :] += y
        if do_down:
            car_a[f - 1] = tail_a
            car_l[f - 1] = tail_l

    @pl.when(f == 0)
    def _():
        h_ref[...] = _norm_mod(x_ref[0], g_ref[0], mod_ref[0, 4:5, :], mod_ref[0, 3:4, :]).astype(BF16)
        run(True, False, False)

    @pl.when(jnp.logical_and(f > 0, t == 0))
    def _():
        car_a[f - 1] = jnp.zeros(car_a.shape[1:], F32)
        car_l[f - 1] = jnp.zeros(car_l.shape[1:], F32)

    @pl.when(f == 1)
    def _():
        run(True, True, True)

    @pl.when(jnp.logical_and(f > 1, f < nf))
    def _():
        run(True, True, False)

    @pl.when(f == nf)
    def _():
        run(False, True, False)
        y = x_ref[0] + mod_ref[0, 5:6, :] * o_ref[0]
        if final:
            ms = jnp.mean(y * y, axis=-1, keepdims=True)
            y = (y * lax.rsqrt(ms + EPS)) * fg_ref[...]
        o_ref[0] = y


def _ffn_call(x, mod, norm_g, w_up, ffn_conv_w, ffn_conv_b, w_down, final_g, layer, *, ts, tf, rc, final):
    bsz, seq, d = x.shape
    n_layers, d_ff, _ = w_down.shape
    kw = ffn_conv_w.shape[1]
    nf = d_ff // tf
    cb = ffn_conv_b.reshape(n_layers, 1, 2 * d_ff)
    up = lambda f: jnp.minimum(f, nf - 1)
    dn = lambda f: jnp.maximum(f - 1, 0)
    kern = functools.partial(_ffn_kernel, ts=ts, rc=rc, nf=nf, conv_w=kw, final=final)
    return pl.pallas_call(
        kern,
        grid=(bsz, seq // ts, nf + 1),
        in_specs=[
            pl.BlockSpec((1, ts, d), lambda b, t, f: (b, t, 0), pipeline_mode=pl.Buffered(1)),
            pl.BlockSpec((1, N_ADA, d), lambda b, t, f: (b, 0, 0)),
            pl.BlockSpec((1, 1, d), lambda b, t, f: (layer, 0, 0)),
            pl.BlockSpec((1, d, tf), lambda b, t, f: (layer, 0, up(f))),
            pl.BlockSpec((1, d, tf), lambda b, t, f: (layer, 0, nf + up(f))),
            pl.BlockSpec((1, kw, tf), lambda b, t, f: (layer, 0, dn(f))),
            pl.BlockSpec((1, kw, tf), lambda b, t, f: (layer, 0, nf + dn(f))),
            pl.BlockSpec((1, 1, tf), lambda b, t, f: (layer, 0, dn(f))),
            pl.BlockSpec((1, 1, tf), lambda b, t, f: (layer, 0, nf + dn(f))),
            pl.BlockSpec((1, tf, d), lambda b, t, f: (layer, dn(f), 0)),
            pl.BlockSpec((1, d), lambda b, t, f: (0, 0)),
        ],
        out_specs=pl.BlockSpec((1, ts, d), lambda b, t, f: (b, t, 0)),
        out_shape=jax.ShapeDtypeStruct((bsz, seq, d), F32),
        scratch_shapes=[
            pltpu.VMEM((ts, d), BF16),
            pltpu.VMEM((ts, tf), F32),
            pltpu.VMEM((ts, tf), F32),
            pltpu.VMEM((nf, SUBLANES, tf), F32),
            pltpu.VMEM((nf, SUBLANES, tf), F32),
        ],
        compiler_params=pltpu.CompilerParams(
            dimension_semantics=("arbitrary", "arbitrary", "arbitrary"), vmem_limit_bytes=VMEM_LIMIT),
        name="conv_ffn",
    )(x, mod, norm_g.reshape(n_layers, 1, d), w_up, w_up, ffn_conv_w, ffn_conv_w, cb, cb,
      w_down, final_g.reshape(1, d))


def kernel(x, c, ada_w, ada_b, norm_mix_g, w_in, b_in, conv_w, conv_b, lru_wa, lru_ba, lru_wx, lru_bx, lru_lambda, pool_w, pool_b, pool_scale, proj_a, proj_b, w_out, norm_ffn_g, w_up, ffn_conv_w, ffn_conv_b, w_down, final_g):
    bsz, seq, d = x.shape
    n_layers = ada_w.shape[0]
    d_rnn = conv_w.shape[-1]
    d_pool = pool_b.shape[-1]
    assert lru_wa.shape[1] == LRU_HEADS and pool_w.shape[1] == len(POOL_WINDOWS)
    assert ada_w.shape[-1] == N_ADA * d

    c_pad = jnp.pad(c, ((0, SUBLANES - bsz), (0, 0)))
    mod_all = _ada_call(c_pad, ada_w, ada_b)[:, :bsz].reshape(n_layers, bsz, N_ADA, d)

    ts = 1024
    for l in range(n_layers):
        mod = mod_all[l]
        zx, zg = _inproj_call(x, mod, norm_mix_g, w_in, b_in, l, d_rnn=d_rnn, d_pool=d_pool, ts=ts, tn=1024)
        ya = _rglru_call(zx, zg, conv_w, conv_b, lru_wa, lru_ba, lru_wx, lru_bx, lru_lambda, l,
                         d_rnn=d_rnn, ts=ts)
        yb = _pool_call(zx, pool_w, pool_b, pool_scale, l, d_rnn=d_rnn, d_pool=d_pool, ts=ts)
        x = _merge_call(ya, yb, zg, proj_a, proj_b, w_out, x, mod, l, ts=ts, bn=256)
        x = _ffn_call(x, mod, norm_ffn_g, w_up, ffn_conv_w, ffn_conv_b, w_down, final_g, l,
                      ts=ts, tf=256, rc=256, final=(l == n_layers - 1))
    return x
```

```python
import functools

import jax
import jax.numpy as jnp
from jax import lax
from jax.experimental import pallas as pl
from jax.experimental.pallas import tpu as pltpu

F32 = jnp.float32
BF16 = jnp.bfloat16

LRU_HEADS = 8
LRU_C = 8.0
POOL_WINDOWS = (2, 4, 8, 16)
N_ADA = 6
EPS = 1e-6
GELU_C0 = 0.7978845608028654
GELU_C1 = 0.044715

SUBLANES = 8
POOL_HALO = 16
VMEM_LIMIT = 56 * 1024 * 1024


def _sigmoid(x):
    return 1.0 / (1.0 + jnp.exp(-x))


def _gelu_tanh(x):
    return 0.5 * x * (1.0 + jnp.tanh(GELU_C0 * (x + GELU_C1 * (x * x * x))))


def _log_sigmoid(x):
    return jnp.minimum(x, 0.0) - jnp.log1p(jnp.exp(-jnp.abs(x)))


def _one_minus_exp(z, u):
    one = u == 1.0
    return jnp.where(one, -z, z * (1.0 - u) / jnp.where(one, 1.0, jnp.log(u)))


def _norm_mod(x, g, scale, shift):
    ms = jnp.mean(x * x, axis=-1, keepdims=True)
    y = (x * lax.rsqrt(ms + EPS)) * g
    return y * (1.0 + scale) + shift


def _col_blocks(w, tn):
    n_layers, k, n = w.shape
    return w.astype(BF16).reshape(n_layers, k, n // tn, tn).transpose(0, 2, 1, 3)


def _shift_rows(v, k):
    return pltpu.roll(v, k, 0)


def _ada_kernel(c_ref, w_ref, b_ref, o_ref):
    c = c_ref[...]
    ca = (c * _sigmoid(c)).astype(BF16)
    o_ref[0] = jnp.dot(ca, w_ref[0].astype(BF16), preferred_element_type=F32) + b_ref[0]


def _ada_call(c_pad, ada_w, ada_b):
    n_layers, d, n_out = ada_w.shape
    tn = 2048
    rows = c_pad.shape[0]
    return pl.pallas_call(
        _ada_kernel,
        grid=(n_layers, n_out // tn),
        in_specs=[
            pl.BlockSpec((rows, d), lambda l, n: (0, 0)),
            pl.BlockSpec((1, d, tn), lambda l, n: (l, 0, n)),
            pl.BlockSpec((1, 1, tn), lambda l, n: (l, 0, n)),
        ],
        out_specs=pl.BlockSpec((1, rows, tn), lambda l, n: (l, 0, n)),
        out_shape=jax.ShapeDtypeStruct((n_layers, rows, n_out), F32),
        compiler_params=pltpu.CompilerParams(
            dimension_semantics=("arbitrary", "arbitrary"), vmem_limit_bytes=VMEM_LIMIT),
        name="ada_mod",
    )(c_pad, ada_w, ada_b.reshape(n_layers, 1, n_out))


def _inproj_kernel(x_ref, mod_ref, g_ref, w_ref, b_ref, zx_ref, zg_ref, h_ref, *, n_x, n_gelu):
    n = pl.program_id(2)

    @pl.when(n == 0)
    def _():
        h_ref[...] = _norm_mod(x_ref[0], g_ref[0], mod_ref[0, 1:2, :], mod_ref[0, 0:1, :]).astype(BF16)

    def z_block():
        return jnp.dot(h_ref[...], w_ref[0, 0], preferred_element_type=F32) + b_ref[0]

    @pl.when(n < n_x)
    def _():
        zx_ref[0] = z_block()

    @pl.when(n >= n_x)
    def _():
        z = z_block()
        is_gelu = n < n_x + n_gelu
        arg = jnp.where(is_gelu, GELU_C0 * (z + GELU_C1 * (z * z * z)), 0.5 * z)
        lead = jnp.where(is_gelu, 0.5 * z, 0.5)
        zg_ref[0] = (lead * (1.0 + jnp.tanh(arg))).astype(BF16)


def _inproj_call(x, mod, norm_g, w_in, b_in, layer, *, d_rnn, d_pool, ts, tn):
    bsz, seq, d = x.shape
    n_layers, d_in = b_in.shape
    d_gates = d_in - 2 * d_rnn - d_pool
    nb_rnn, nb_pool, nb_gates = d_rnn // tn, d_pool // tn, d_gates // tn
    n_x = nb_rnn + nb_pool
    n_steps = n_x + nb_rnn + nb_gates

    def w_col(n):
        return jnp.where(n < nb_rnn, n,
                         jnp.where(n < n_x, n + nb_rnn,
                                   jnp.where(n < n_x + nb_rnn, n - nb_pool, n)))

    kern = functools.partial(_inproj_kernel, n_x=n_x, n_gelu=nb_rnn)
    return pl.pallas_call(
        kern,
        grid=(bsz, seq // ts, n_steps),
        in_specs=[
            pl.BlockSpec((1, ts, d), lambda b, t, n: (b, t, 0), pipeline_mode=pl.Buffered(1)),
            pl.BlockSpec((1, N_ADA, d), lambda b, t, n: (b, 0, 0)),
            pl.BlockSpec((1, 1, d), lambda b, t, n: (layer, 0, 0)),
            pl.BlockSpec((1, 1, d, tn), lambda b, t, n: (layer, w_col(n), 0, 0)),
            pl.BlockSpec((1, 1, tn), lambda b, t, n: (layer, 0, w_col(n))),
        ],
        out_specs=[
            pl.BlockSpec((1, ts, tn), lambda b, t, n: (b, t, jnp.minimum(n, n_x - 1))),
            pl.BlockSpec((1, ts, tn), lambda b, t, n: (b, t, jnp.maximum(n - n_x, 0))),
        ],
        out_shape=[
            jax.ShapeDtypeStruct((bsz, seq, d_rnn + d_pool), F32),
            jax.ShapeDtypeStruct((bsz, seq, d_rnn + d_gates), BF16),
        ],
        scratch_shapes=[pltpu.VMEM((ts, d), BF16)],
        compiler_params=pltpu.CompilerParams(
            dimension_semantics=("arbitrary", "arbitrary", "arbitrary"), vmem_limit_bytes=VMEM_LIMIT),
        name="in_proj",
    )(x, mod, norm_g.reshape(n_layers, 1, d), w_in, b_in.reshape(n_layers, 1, d_in))


def _rglru_kernel(xr_ref, gg_ref, cw_ref, cb_ref, wa_ref, ba_ref, wx_ref, bx_ref, lam_ref,
                  ya_ref, xbuf, a_buf, b_buf, h_buf, hcar, *, ts, conv_w):
    t = pl.program_id(2)
    halo = SUBLANES

    @pl.when(t == 0)
    def _():
        xbuf[0:halo, :] = jnp.zeros((halo, xbuf.shape[1]), F32)
        hcar[...] = jnp.zeros(hcar.shape, F32)

    xbuf[halo:halo + ts, :] = xr_ref[0]
    v = xbuf[...]
    cw = cw_ref[0]
    acc = cw[conv_w - 1:conv_w, :] * v
    for k in range(1, conv_w):
        acc = acc + cw[conv_w - 1 - k:conv_w - k, :] * _shift_rows(v, k)
    xr = acc[halo:, :] + cb_ref[0]
    xbuf[0:halo, :] = v[ts:ts + halo, :]

    xrb = xr.astype(BF16)
    r = _sigmoid(jnp.dot(xrb, wa_ref[0, 0].astype(BF16), preferred_element_type=F32) + ba_ref[0])
    i = _sigmoid(jnp.dot(xrb, wx_ref[0, 0].astype(BF16), preferred_element_type=F32) + bx_ref[0])
    log_a = (LRU_C * r) * _log_sigmoid(lam_ref[0])
    a = jnp.exp(log_a)
    b = jnp.sqrt(_one_minus_exp(2.0 * log_a, a * a)) * (i * xr)

    rowm = lax.broadcasted_iota(jnp.int32, a.shape, 0) & (SUBLANES - 1)
    s = 1
    while s < SUBLANES:
        keep = rowm >= s
        a_sh = jnp.where(keep, _shift_rows(a, s), 1.0)
        b_sh = jnp.where(keep, _shift_rows(b, s), 0.0)
        b = a * b_sh + b
        a = a * a_sh
        s *= 2
    a_buf[...] = a
    b_buf[...] = b

    def body(g, hc):
        r0 = pl.multiple_of(g * SUBLANES, SUBLANES)
        h8 = a_buf[pl.ds(r0, SUBLANES), :] * hc + b_buf[pl.ds(r0, SUBLANES), :]
        h_buf[pl.ds(r0, SUBLANES), :] = h8
        return jnp.broadcast_to(h8[SUBLANES - 1:SUBLANES, :], hc.shape)

    hcar[...] = lax.fori_loop(0, ts // SUBLANES, body, hcar[...], unroll=8)
    ya_ref[0] = (h_buf[...] * gg_ref[0].astype(F32)).astype(BF16)


def _rglru_call(zx, zg, conv_w, conv_b, lru_wa, lru_ba, lru_wx, lru_bx, lru_lambda, layer, *, d_rnn, ts):
    bsz, seq, _ = zx.shape
    n_layers, kw, _ = conv_w.shape
    heads, wblk = lru_wa.shape[1], lru_wa.shape[2]
    vec = lambda a: a.reshape(n_layers, 1, d_rnn)
    vspec = pl.BlockSpec((1, 1, wblk), lambda b, j, t: (layer, 0, j))
    mspec = pl.BlockSpec((1, 1, wblk, wblk), lambda b, j, t: (layer, j, 0, 0))
    kern = functools.partial(_rglru_kernel, ts=ts, conv_w=kw)
    return pl.pallas_call(
        kern,
        grid=(bsz, heads, seq // ts),
        in_specs=[
            pl.BlockSpec((1, ts, wblk), lambda b, j, t: (b, t, j)),
            pl.BlockSpec((1, ts, wblk), lambda b, j, t: (b, t, j)),
            pl.BlockSpec((1, kw, wblk), lambda b, j, t: (layer, 0, j)),
            vspec, mspec, vspec, mspec, vspec, vspec,
        ],
        out_specs=pl.BlockSpec((1, ts, wblk), lambda b, j, t: (b, t, j)),
        out_shape=jax.ShapeDtypeStruct((bsz, seq, d_rnn), BF16),
        scratch_shapes=[
            pltpu.VMEM((SUBLANES + ts, wblk), F32),
            pltpu.VMEM((ts, wblk), F32),
            pltpu.VMEM((ts, wblk), F32),
            pltpu.VMEM((ts, wblk), F32),
            pltpu.VMEM((SUBLANES, wblk), F32),
        ],
        compiler_params=pltpu.CompilerParams(
            dimension_semantics=("arbitrary", "arbitrary", "arbitrary"), vmem_limit_bytes=VMEM_LIMIT),
        name="rg_lru",
    )(zx, zg, conv_w, vec(conv_b), lru_wa, vec(lru_ba), lru_wx, vec(lru_bx), vec(lru_lambda))


def _pool_kernel(xp_ref, w_ref, b_ref, s_ref, yb_ref, xbuf, *, ts, gw):
    t = pl.program_id(1)
    halo = POOL_HALO

    @pl.when(t == 0)
    def _():
        xbuf[0:halo, :] = jnp.zeros((halo, xbuf.shape[1]), F32)

    xbuf[halo:halo + ts, :] = xp_ref[0]
    pos = (t * ts + lax.broadcasted_iota(jnp.int32, (ts, gw), 0)).astype(F32)
    for g, win in enumerate(POOL_WINDOWS):
        v = xbuf[:, g * gw:(g + 1) * gw]
        sm = v
        k = 1
        while k < win:
            sm = sm + _shift_rows(sm, k)
            k *= 2
        cur = v[halo:, :]
        count = jnp.minimum(pos + 1.0, float(win))
        p = (sm[halo:, :] / count - cur).astype(BF16)
        y = jnp.dot(p, w_ref[0, g].astype(BF16), preferred_element_type=F32) + b_ref[0, :, g * gw:(g + 1) * gw]
        yb_ref[0, :, g * gw:(g + 1) * gw] = (y * s_ref[0, :, g * gw:(g + 1) * gw]).astype(BF16)
    xbuf[0:halo, :] = xbuf[ts:ts + halo, :]


def _pool_call(zx, pool_w, pool_b, pool_scale, layer, *, d_rnn, d_pool, ts):
    bsz, seq, _ = zx.shape
    n_layers, groups, gw, _ = pool_w.shape
    col0 = d_rnn // d_pool
    vec = lambda a: a.reshape(n_layers, 1, d_pool)
    vspec = pl.BlockSpec((1, 1, d_pool), lambda b, t: (layer, 0, 0))
    kern = functools.partial(_pool_kernel, ts=ts, gw=gw)
    return pl.pallas_call(
        kern,
        grid=(bsz, seq // ts),
        in_specs=[
            pl.BlockSpec((1, ts, d_pool), lambda b, t: (b, t, col0)),
            pl.BlockSpec((1, groups, gw, gw), lambda b, t: (layer, 0, 0, 0)),
            vspec, vspec,
        ],
        out_specs=pl.BlockSpec((1, ts, d_pool), lambda b, t: (b, t, 0)),
        out_shape=jax.ShapeDtypeStruct((bsz, seq, d_pool), BF16),
        scratch_shapes=[pltpu.VMEM((POOL_HALO + ts, d_pool), F32)],
        compiler_params=pltpu.CompilerParams(
            dimension_semantics=("arbitrary", "arbitrary"), vmem_limit_bytes=VMEM_LIMIT),
        name="ms_pool",
    )(zx, pool_w, vec(pool_b), vec(pool_scale))


def _merge_kernel(ya_ref, yb_ref, ga_ref, gb_ref, pa_ref, pb_ref, wo_ref, x_ref, mod_ref,
                  o_ref, m_ref, *, nb, bn):
    n = pl.program_id(2)

    @pl.when(n < nb)
    def _():
        pa = jnp.dot(ya_ref[0], pa_ref[0, 0], preferred_element_type=F32)
        pb = jnp.dot(yb_ref[0], pb_ref[0, 0], preferred_element_type=F32)
        merged = ga_ref[0].astype(F32) * pa + gb_ref[0].astype(F32) * pb
        c0 = pl.multiple_of(n * bn, bn)
        m_ref[:, pl.ds(c0, bn)] = merged.astype(BF16)

    @pl.when(n >= nb)
    def _():
        y = jnp.dot(m_ref[...], wo_ref[0, 0], preferred_element_type=F32)
        o_ref[0] = x_ref[0] + mod_ref[0, 2:3, :] * y


def _merge_call(ya, yb, zg, proj_a, proj_b, w_out, x, mod, layer, *, ts, bn):
    bsz, seq, d = x.shape
    d_rnn, d_pool = ya.shape[-1], yb.shape[-1]
    nb = d // bn
    ga0 = d_rnn // bn
    gb0 = ga0 + nb
    first = lambda n: jnp.minimum(n, nb - 1)
    second = lambda n: jnp.maximum(n - nb, 0)
    kern = functools.partial(_merge_kernel, nb=nb, bn=bn)
    return pl.pallas_call(
        kern,
        grid=(bsz, seq // ts, 2 * nb),
        in_specs=[
            pl.BlockSpec((1, ts, d_rnn), lambda b, t, n: (b, t, 0)),
            pl.BlockSpec((1, ts, d_pool), lambda b, t, n: (b, t, 0)),
            pl.BlockSpec((1, ts, bn), lambda b, t, n: (b, t, ga0 + first(n))),
            pl.BlockSpec((1, ts, bn), lambda b, t, n: (b, t, gb0 + first(n))),
            pl.BlockSpec((1, 1, d_rnn, bn), lambda b, t, n: (layer, first(n), 0, 0)),
            pl.BlockSpec((1, 1, d_pool, bn), lambda b, t, n: (layer, first(n), 0, 0)),
            pl.BlockSpec((1, 1, d, bn), lambda b, t, n: (layer, second(n), 0, 0)),
            pl.BlockSpec((1, ts, bn), lambda b, t, n: (b, t, second(n))),
            pl.BlockSpec((1, N_ADA, bn), lambda b, t, n: (b, 0, second(n))),
        ],
        out_specs=pl.BlockSpec((1, ts, bn), lambda b, t, n: (b, t, second(n))),
        out_shape=jax.ShapeDtypeStruct((bsz, seq, d), F32),
        scratch_shapes=[pltpu.VMEM((ts, d), BF16)],
        compiler_params=pltpu.CompilerParams(
            dimension_semantics=("arbitrary", "arbitrary", "arbitrary"), vmem_limit_bytes=VMEM_LIMIT),
        name="merge",
    )(ya, yb, zg, zg, proj_a, proj_b, w_out, x, mod)


def _ffn_kernel(x_ref, mod_ref, g_ref, wua_ref, wul_ref, cwa_ref, cwl_ref, cba_ref, cbl_ref,
                wd_ref, fg_ref, o_ref, h_ref, ua_buf, ul_buf, car_a, car_l, *, ts, rc, nf, conv_w, final):
    t = pl.program_id(1)
    f = pl.program_id(2)
    halo = SUBLANES
    n_chunks = ts // rc

    def conv(v, cw_ref, cb_ref):
        cw = cw_ref[0]
        acc = cw[conv_w - 1:conv_w, :] * v
        for k in range(1, conv_w):
            acc = acc + cw[conv_w - 1 - k:conv_w - k, :] * _shift_rows(v, k)
        return acc[halo:, :] + cb_ref[0]

    def run(do_up, do_down, first):
        if do_up:
            wa = wua_ref[0, 0]
            wl = wul_ref[0, 0]
        if do_down:
            wd = wd_ref[0]
            tail_a = car_a[f - 1]
            tail_l = car_l[f - 1]
        for c in range(n_chunks):
            rows = slice(c * rc, (c + 1) * rc)
            if do_down:
                ua = ua_buf[rows, :]
                ul = ul_buf[rows, :]
                ca = conv(jnp.concatenate([tail_a, ua], axis=0), cwa_ref, cba_ref)
                cl = conv(jnp.concatenate([tail_l, ul], axis=0), cwl_ref, cbl_ref)
                p = ((ca * _sigmoid(ca)) * cl).astype(BF16)
                tail_a = ua[rc - halo:, :]
                tail_l = ul[rc - halo:, :]
            if do_up:
                h = h_ref[rows, :]
                ua_buf[rows, :] = jnp.dot(h, wa, preferred_element_type=F32)
                ul_buf[rows, :] = jnp.dot(h, wl, preferred_element_type=F32)
            if do_down:
                y = jnp.dot(p, wd, preferred_element_type=F32)
                if first:
                    o_ref[0, rows, :] = y
                else:
                    o_ref[0, rows, :] += y
        if do_down:
            car_a[f - 1] = tail_a
            car_l[f - 1] = tail_l

    @pl.when(f == 0)
    def _():
        h_ref[...] = _norm_mod(x_ref[0], g_ref[0], mod_ref[0, 4:5, :], mod_ref[0, 3:4, :]).astype(BF16)
        run(True, False, False)

    @pl.when(jnp.logical_and(f > 0, t == 0))
    def _():
        car_a[f - 1] = jnp.zeros(car_a.shape[1:], F32)
        car_l[f - 1] = jnp.zeros(car_l.shape[1:], F32)

    @pl.when(f == 1)
    def _():
        run(True, True, True)

    @pl.when(jnp.logical_and(f > 1, f < nf))
    def _():
        run(True, True, False)

    @pl.when(f == nf)
    def _():
        run(False, True, False)
        y = x_ref[0] + mod_ref[0, 5:6, :] * o_ref[0]
        if final:
            ms = jnp.mean(y * y, axis=-1, keepdims=True)
            y = (y * lax.rsqrt(ms + EPS)) * fg_ref[...]
        o_ref[0] = y


def _ffn_call(x, mod, norm_g, w_up, ffn_conv_w, ffn_conv_b, w_down, final_g, layer, *, ts, tf, rc, final):
    bsz, seq, d = x.shape
    n_layers, d_ff, _ = w_down.shape
    kw = ffn_conv_w.shape[1]
    nf = d_ff // tf
    cb = ffn_conv_b.reshape(n_layers, 1, 2 * d_ff)
    up = lambda f: jnp.minimum(f, nf - 1)
    dn = lambda f: jnp.maximum(f - 1, 0)
    kern = functools.partial(_ffn_kernel, ts=ts, rc=rc, nf=nf, conv_w=kw, final=final)
    return pl.pallas_call(
        kern,
        grid=(bsz, seq // ts, nf + 1),
        in_specs=[
            pl.BlockSpec((1, ts, d), lambda b, t, f: (b, t, 0), pipeline_mode=pl.Buffered(1)),
            pl.BlockSpec((1, N_ADA, d), lambda b, t, f: (b, 0, 0)),
            pl.BlockSpec((1, 1, d), lambda b, t, f: (layer, 0, 0)),
            pl.BlockSpec((1, 1, d, tf), lambda b, t, f: (layer, up(f), 0, 0)),
            pl.BlockSpec((1, 1, d, tf), lambda b, t, f: (layer, nf + up(f), 0, 0)),
            pl.BlockSpec((1, kw, tf), lambda b, t, f: (layer, 0, dn(f))),
            pl.BlockSpec((1, kw, tf), lambda b, t, f: (layer, 0, nf + dn(f))),
            pl.BlockSpec((1, 1, tf), lambda b, t, f: (layer, 0, dn(f))),
            pl.BlockSpec((1, 1, tf), lambda b, t, f: (layer, 0, nf + dn(f))),
            pl.BlockSpec((1, tf, d), lambda b, t, f: (layer, dn(f), 0)),
            pl.BlockSpec((1, d), lambda b, t, f: (0, 0)),
        ],
        out_specs=pl.BlockSpec((1, ts, d), lambda b, t, f: (b, t, 0)),
        out_shape=jax.ShapeDtypeStruct((bsz, seq, d), F32),
        scratch_shapes=[
            pltpu.VMEM((ts, d), BF16),
            pltpu.VMEM((ts, tf), F32),
            pltpu.VMEM((ts, tf), F32),
            pltpu.VMEM((nf, SUBLANES, tf), F32),
            pltpu.VMEM((nf, SUBLANES, tf), F32),
        ],
        compiler_params=pltpu.CompilerParams(
            dimension_semantics=("arbitrary", "arbitrary", "arbitrary"), vmem_limit_bytes=VMEM_LIMIT),
        name="conv_ffn",
    )(x, mod, norm_g.reshape(n_layers, 1, d), w_up, w_up, ffn_conv_w, ffn_conv_w, cb, cb,
      w_down, final_g.reshape(1, d))


def kernel(x, c, ada_w, ada_b, norm_mix_g, w_in, b_in, conv_w, conv_b, lru_wa, lru_ba, lru_wx, lru_bx, lru_lambda, pool_w, pool_b, pool_scale, proj_a, proj_b, w_out, norm_ffn_g, w_up, ffn_conv_w, ffn_conv_b, w_down, final_g):
    bsz, seq, d = x.shape
    n_layers = ada_w.shape[0]
    d_rnn = conv_w.shape[-1]
    d_pool = pool_b.shape[-1]
    assert lru_wa.shape[1] == LRU_HEADS and pool_w.shape[1] == len(POOL_WINDOWS)
    assert ada_w.shape[-1] == N_ADA * d

    c_pad = jnp.pad(c, ((0, SUBLANES - bsz), (0, 0)))
    mod_all = _ada_call(c_pad, ada_w, ada_b)[:, :bsz].reshape(n_layers, bsz, N_ADA, d)

    ts, tn, bn, tf = 1024, 1024, 512, 512
    w_in_b = _col_blocks(w_in, tn)
    proj_a_b, proj_b_b, w_out_b = _col_blocks(proj_a, bn), _col_blocks(proj_b, bn), _col_blocks(w_out, bn)
    w_up_b = _col_blocks(w_up, tf)
    w_down_b = w_down.astype(BF16)
    for l in range(n_layers):
        mod = mod_all[l]
        zx, zg = _inproj_call(x, mod, norm_mix_g, w_in_b, b_in, l, d_rnn=d_rnn, d_pool=d_pool, ts=ts, tn=tn)
        ya = _rglru_call(zx, zg, conv_w, conv_b, lru_wa, lru_ba, lru_wx, lru_bx, lru_lambda, l,
                         d_rnn=d_rnn, ts=ts)
        yb = _pool_call(zx, pool_w, pool_b, pool_scale, l, d_rnn=d_rnn, d_pool=d_pool, ts=ts)
        x = _merge_call(ya, yb, zg, proj_a_b, proj_b_b, w_out_b, x, mod, l, ts=ts, bn=bn)
        x = _ffn_call(x, mod, norm_ffn_g, w_up_b, ffn_conv_w, ffn_conv_b, w_down_b, final_g, l,
                      ts=ts, tf=tf, rc=256, final=(l == n_layers - 1))
    return x
```

```python
import functools

import jax
import jax.numpy as jnp
from jax import lax
from jax.experimental import pallas as pl
from jax.experimental.pallas import tpu as pltpu

F32 = jnp.float32
BF16 = jnp.bfloat16

LRU_HEADS = 8
LRU_C = 8.0
POOL_WINDOWS = (2, 4, 8, 16)
N_ADA = 6
EPS = 1e-6
GELU_C0 = 0.7978845608028654
GELU_C1 = 0.044715

SUBLANES = 8
POOL_HALO = 16
VMEM_LIMIT = 56 * 1024 * 1024


def _sigmoid(x):
    return 0.5 * (1.0 + jnp.tanh(0.5 * x))


def _gelu_tanh(x):
    return 0.5 * x * (1.0 + jnp.tanh(GELU_C0 * (x + GELU_C1 * (x * x * x))))


def _log_sigmoid(x):
    return jnp.minimum(x, 0.0) - jnp.log1p(jnp.exp(-jnp.abs(x)))


def _norm_mod(x, g, scale, shift):
    ms = jnp.mean(x * x, axis=-1, keepdims=True)
    y = (x * lax.rsqrt(ms + EPS)) * g
    return y * (1.0 + scale) + shift


def _col_blocks(w, tn):
    n_layers, k, n = w.shape
    return w.astype(BF16).reshape(n_layers, k, n // tn, tn).transpose(0, 2, 1, 3)


def _shift_rows(v, k):
    return pltpu.roll(v, k, 0)


def _ada_kernel(c_ref, w_ref, b_ref, o_ref):
    c = c_ref[...]
    ca = (c * _sigmoid(c)).astype(BF16)
    o_ref[0] = jnp.dot(ca, w_ref[0].astype(BF16), preferred_element_type=F32) + b_ref[0]


def _ada_call(c_pad, ada_w, ada_b):
    n_layers, d, n_out = ada_w.shape
    tn = 2048
    rows = c_pad.shape[0]
    return pl.pallas_call(
        _ada_kernel,
        grid=(n_layers, n_out // tn),
        in_specs=[
            pl.BlockSpec((rows, d), lambda l, n: (0, 0)),
            pl.BlockSpec((1, d, tn), lambda l, n: (l, 0, n)),
            pl.BlockSpec((1, 1, tn), lambda l, n: (l, 0, n)),
        ],
        out_specs=pl.BlockSpec((1, rows, tn), lambda l, n: (l, 0, n)),
        out_shape=jax.ShapeDtypeStruct((n_layers, rows, n_out), F32),
        compiler_params=pltpu.CompilerParams(
            dimension_semantics=("arbitrary", "arbitrary"), vmem_limit_bytes=VMEM_LIMIT),
        name="ada_mod",
    )(c_pad, ada_w, ada_b.reshape(n_layers, 1, n_out))


def _inproj_kernel(x_ref, mod_ref, g_ref, w_ref, b_ref, zx_ref, zg_ref, h_ref, *, n_x, n_gelu):
    n = pl.program_id(2)

    @pl.when(n == 0)
    def _():
        h_ref[...] = _norm_mod(x_ref[0], g_ref[0], mod_ref[0, 1:2, :], mod_ref[0, 0:1, :]).astype(BF16)

    def z_block():
        return jnp.dot(h_ref[...], w_ref[0].astype(BF16), preferred_element_type=F32) + b_ref[0]

    @pl.when(n < n_x)
    def _():
        zx_ref[0] = z_block()

    @pl.when(n >= n_x)
    def _():
        z = z_block()
        is_gelu = n < n_x + n_gelu
        arg = jnp.where(is_gelu, GELU_C0 * (z + GELU_C1 * (z * z * z)), 0.5 * z)
        lead = jnp.where(is_gelu, 0.5 * z, 0.5)
        zg_ref[0] = (lead * (1.0 + jnp.tanh(arg))).astype(BF16)


def _inproj_call(x, mod, norm_g, w_in, b_in, layer, *, d_rnn, d_pool, ts, tn):
    bsz, seq, d = x.shape
    n_layers, d_in = b_in.shape
    d_gates = d_in - 2 * d_rnn - d_pool
    nb_rnn, nb_pool, nb_gates = d_rnn // tn, d_pool // tn, d_gates // tn
    n_x = nb_rnn + nb_pool
    n_steps = n_x + nb_rnn + nb_gates

    def w_col(n):
        return jnp.where(n < nb_rnn, n,
                         jnp.where(n < n_x, n + nb_rnn,
                                   jnp.where(n < n_x + nb_rnn, n - nb_pool, n)))

    kern = functools.partial(_inproj_kernel, n_x=n_x, n_gelu=nb_rnn)
    return pl.pallas_call(
        kern,
        grid=(bsz, seq // ts, n_steps),
        in_specs=[
            pl.BlockSpec((1, ts, d), lambda b, t, n: (b, t, 0), pipeline_mode=pl.Buffered(1)),
            pl.BlockSpec((1, N_ADA, d), lambda b, t, n: (b, 0, 0)),
            pl.BlockSpec((1, 1, d), lambda b, t, n: (layer, 0, 0)),
            pl.BlockSpec((1, d, tn), lambda b, t, n: (layer, 0, w_col(n))),
            pl.BlockSpec((1, 1, tn), lambda b, t, n: (layer, 0, w_col(n))),
        ],
        out_specs=[
            pl.BlockSpec((1, ts, tn), lambda b, t, n: (b, t, jnp.minimum(n, n_x - 1))),
            pl.BlockSpec((1, ts, tn), lambda b, t, n: (b, t, jnp.maximum(n - n_x, 0))),
        ],
        out_shape=[
            jax.ShapeDtypeStruct((bsz, seq, d_rnn + d_pool), F32),
            jax.ShapeDtypeStruct((bsz, seq, d_rnn + d_gates), BF16),
        ],
        scratch_shapes=[pltpu.VMEM((ts, d), BF16)],
        compiler_params=pltpu.CompilerParams(
            dimension_semantics=("arbitrary", "arbitrary", "arbitrary"), vmem_limit_bytes=VMEM_LIMIT),
        name="in_proj",
    )(x, mod, norm_g.reshape(n_layers, 1, d), w_in, b_in.reshape(n_layers, 1, d_in))


def _rglru_kernel(xr_ref, gg_ref, cw_ref, cb_ref, wa_ref, ba_ref, wx_ref, bx_ref, lam_ref,
                  ya_ref, xbuf, a_buf, b_buf, h_buf, hcar, *, ts, conv_w):
    t = pl.program_id(2)
    halo = SUBLANES

    @pl.when(t == 0)
    def _():
        xbuf[0:halo, :] = jnp.zeros((halo, xbuf.shape[1]), F32)
        hcar[...] = jnp.zeros(hcar.shape, F32)

    xbuf[halo:halo + ts, :] = xr_ref[0]
    v = xbuf[...]
    cw = cw_ref[0]
    acc = cw[conv_w - 1:conv_w, :] * v
    for k in range(1, conv_w):
        acc = acc + cw[conv_w - 1 - k:conv_w - k, :] * _shift_rows(v, k)
    xr = acc[halo:, :] + cb_ref[0]
    xbuf[0:halo, :] = v[ts:ts + halo, :]

    xrb = xr.astype(BF16)
    r = _sigmoid(jnp.dot(xrb, wa_ref[0, 0].astype(BF16), preferred_element_type=F32) + ba_ref[0])
    i = _sigmoid(jnp.dot(xrb, wx_ref[0, 0].astype(BF16), preferred_element_type=F32) + bx_ref[0])
    log_a = (LRU_C * r) * _log_sigmoid(lam_ref[0])
    a = jnp.exp(log_a)
    b = jnp.sqrt(1.0 - a * a) * (i * xr)

    grouped = (ts // SUBLANES, SUBLANES, a.shape[-1])
    a = a.reshape(grouped)
    b = b.reshape(grouped)
    rowm = lax.broadcasted_iota(jnp.int32, grouped, 1)
    s = 1
    while s < SUBLANES:
        keep = rowm >= s
        a_sh = jnp.where(keep, pltpu.roll(a, s, 1), 1.0)
        b_sh = jnp.where(keep, pltpu.roll(b, s, 1), 0.0)
        b = a * b_sh + b
        a = a * a_sh
        s *= 2
    a_buf[...] = a.reshape(ts, grouped[-1])
    b_buf[...] = b.reshape(ts, grouped[-1])

    def body(g, hc):
        r0 = pl.multiple_of(g * SUBLANES, SUBLANES)
        h8 = a_buf[pl.ds(r0, SUBLANES), :] * hc + b_buf[pl.ds(r0, SUBLANES), :]
        h_buf[pl.ds(r0, SUBLANES), :] = h8
        return jnp.broadcast_to(h8[SUBLANES - 1:SUBLANES, :], hc.shape)

    hcar[...] = lax.fori_loop(0, ts // SUBLANES, body, hcar[...], unroll=8)
    ya_ref[0] = (h_buf[...] * gg_ref[0].astype(F32)).astype(BF16)


def _rglru_call(zx, zg, conv_w, conv_b, lru_wa, lru_ba, lru_wx, lru_bx, lru_lambda, layer, *, d_rnn, ts):
    bsz, seq, _ = zx.shape
    n_layers, kw, _ = conv_w.shape
    heads, wblk = lru_wa.shape[1], lru_wa.shape[2]
    vec = lambda a: a.reshape(n_layers, 1, d_rnn)
    vspec = pl.BlockSpec((1, 1, wblk), lambda b, j, t: (layer, 0, j))
    mspec = pl.BlockSpec((1, 1, wblk, wblk), lambda b, j, t: (layer, j, 0, 0))
    kern = functools.partial(_rglru_kernel, ts=ts, conv_w=kw)
    return pl.pallas_call(
        kern,
        grid=(bsz, heads, seq // ts),
        in_specs=[
            pl.BlockSpec((1, ts, wblk), lambda b, j, t: (b, t, j)),
            pl.BlockSpec((1, ts, wblk), lambda b, j, t: (b, t, j)),
            pl.BlockSpec((1, kw, wblk), lambda b, j, t: (layer, 0, j)),
            vspec, mspec, vspec, mspec, vspec, vspec,
        ],
        out_specs=pl.BlockSpec((1, ts, wblk), lambda b, j, t: (b, t, j)),
        out_shape=jax.ShapeDtypeStruct((bsz, seq, d_rnn), BF16),
        scratch_shapes=[
            pltpu.VMEM((SUBLANES + ts, wblk), F32),
            pltpu.VMEM((ts, wblk), F32),
            pltpu.VMEM((ts, wblk), F32),
            pltpu.VMEM((ts, wblk), F32),
            pltpu.VMEM((SUBLANES, wblk), F32),
        ],
        compiler_params=pltpu.CompilerParams(
            dimension_semantics=("arbitrary", "arbitrary", "arbitrary"), vmem_limit_bytes=VMEM_LIMIT),
        name="rg_lru",
    )(zx, zg, conv_w, vec(conv_b), lru_wa, vec(lru_ba), lru_wx, vec(lru_bx), vec(lru_lambda))


def _pool_kernel(xp_ref, w_ref, b_ref, s_ref, yb_ref, xbuf, *, ts, gw):
    t = pl.program_id(1)
    halo = POOL_HALO

    @pl.when(t == 0)
    def _():
        xbuf[0:halo, :] = jnp.zeros((halo, xbuf.shape[1]), F32)

    xbuf[halo:halo + ts, :] = xp_ref[0]
    pos = (t * ts + lax.broadcasted_iota(jnp.int32, (ts, gw), 0)).astype(F32)
    for g, win in enumerate(POOL_WINDOWS):
        v = xbuf[:, g * gw:(g + 1) * gw]
        sm = v
        k = 1
        while k < win:
            sm = sm + _shift_rows(sm, k)
            k *= 2
        cur = v[halo:, :]
        count = jnp.minimum(pos + 1.0, float(win))
        p = (sm[halo:, :] / count - cur).astype(BF16)
        y = jnp.dot(p, w_ref[0, g].astype(BF16), preferred_element_type=F32) + b_ref[0, :, g * gw:(g + 1) * gw]
        yb_ref[0, :, g * gw:(g + 1) * gw] = (y * s_ref[0, :, g * gw:(g + 1) * gw]).astype(BF16)
    xbuf[0:halo, :] = xbuf[ts:ts + halo, :]


def _pool_call(zx, pool_w, pool_b, pool_scale, layer, *, d_rnn, d_pool, ts):
    bsz, seq, _ = zx.shape
    n_layers, groups, gw, _ = pool_w.shape
    col0 = d_rnn // d_pool
    vec = lambda a: a.reshape(n_layers, 1, d_pool)
    vspec = pl.BlockSpec((1, 1, d_pool), lambda b, t: (layer, 0, 0))
    kern = functools.partial(_pool_kernel, ts=ts, gw=gw)
    return pl.pallas_call(
        kern,
        grid=(bsz, seq // ts),
        in_specs=[
            pl.BlockSpec((1, ts, d_pool), lambda b, t: (b, t, col0)),
            pl.BlockSpec((1, groups, gw, gw), lambda b, t: (layer, 0, 0, 0)),
            vspec, vspec,
        ],
        out_specs=pl.BlockSpec((1, ts, d_pool), lambda b, t: (b, t, 0)),
        out_shape=jax.ShapeDtypeStruct((bsz, seq, d_pool), BF16),
        scratch_shapes=[pltpu.VMEM((POOL_HALO + ts, d_pool), F32)],
        compiler_params=pltpu.CompilerParams(
            dimension_semantics=("arbitrary", "arbitrary"), vmem_limit_bytes=VMEM_LIMIT),
        name="ms_pool",
    )(zx, pool_w, vec(pool_b), vec(pool_scale))


def _merge_kernel(ya_ref, yb_ref, ga_ref, gb_ref, pa_ref, pb_ref, wo_ref, x_ref, mod_ref,
                  o_ref, m_ref, *, nb, bn):
    n = pl.program_id(2)

    @pl.when(n < nb)
    def _():
        pa = jnp.dot(ya_ref[0], pa_ref[0].astype(BF16), preferred_element_type=F32)
        pb = jnp.dot(yb_ref[0], pb_ref[0].astype(BF16), preferred_element_type=F32)
        merged = ga_ref[0].astype(F32) * pa + gb_ref[0].astype(F32) * pb
        c0 = pl.multiple_of(n * bn, bn)
        m_ref[:, pl.ds(c0, bn)] = merged.astype(BF16)

    @pl.when(n >= nb)
    def _():
        y = jnp.dot(m_ref[...], wo_ref[0].astype(BF16), preferred_element_type=F32)
        o_ref[0] = x_ref[0] + mod_ref[0, 2:3, :] * y


def _merge_call(ya, yb, zg, proj_a, proj_b, w_out, x, mod, layer, *, ts, bn):
    bsz, seq, d = x.shape
    d_rnn, d_pool = ya.shape[-1], yb.shape[-1]
    nb = d // bn
    ga0 = d_rnn // bn
    gb0 = ga0 + nb
    first = lambda n: jnp.minimum(n, nb - 1)
    second = lambda n: jnp.maximum(n - nb, 0)
    kern = functools.partial(_merge_kernel, nb=nb, bn=bn)
    return pl.pallas_call(
        kern,
        grid=(bsz, seq // ts, 2 * nb),
        in_specs=[
            pl.BlockSpec((1, ts, d_rnn), lambda b, t, n: (b, t, 0)),
            pl.BlockSpec((1, ts, d_pool), lambda b, t, n: (b, t, 0)),
            pl.BlockSpec((1, ts, bn), lambda b, t, n: (b, t, ga0 + first(n))),
            pl.BlockSpec((1, ts, bn), lambda b, t, n: (b, t, gb0 + first(n))),
            pl.BlockSpec((1, d_rnn, bn), lambda b, t, n: (layer, 0, first(n))),
            pl.BlockSpec((1, d_pool, bn), lambda b, t, n: (layer, 0, first(n))),
            pl.BlockSpec((1, d, bn), lambda b, t, n: (layer, 0, second(n))),
            pl.BlockSpec((1, ts, bn), lambda b, t, n: (b, t, second(n))),
            pl.BlockSpec((1, N_ADA, bn), lambda b, t, n: (b, 0, second(n))),
        ],
        out_specs=pl.BlockSpec((1, ts, bn), lambda b, t, n: (b, t, second(n))),
        out_shape=jax.ShapeDtypeStruct((bsz, seq, d), F32),
        scratch_shapes=[pltpu.VMEM((ts, d), BF16)],
        compiler_params=pltpu.CompilerParams(
            dimension_semantics=("arbitrary", "arbitrary", "arbitrary"), vmem_limit_bytes=VMEM_LIMIT),
        name="merge",
    )(ya, yb, zg, zg, proj_a, proj_b, w_out, x, mod)


def _ffn_kernel(x_ref, mod_ref, g_ref, wua_ref, wul_ref, cwa_ref, cwl_ref, cba_ref, cbl_ref,
                wd_ref, fg_ref, o_ref, h_ref, ua_buf, ul_buf, car_a, car_l, *, ts, rc, nf, conv_w, final):
    t = pl.program_id(1)
    f = pl.program_id(2)
    halo = SUBLANES
    n_chunks = ts // rc

    def conv(v, cw_ref, cb_ref):
        cw = cw_ref[0]
        acc = cw[conv_w - 1:conv_w, :] * v
        for k in range(1, conv_w):
            acc = acc + cw[conv_w - 1 - k:conv_w - k, :] * _shift_rows(v, k)
        return acc[halo:, :] + cb_ref[0]

    def run(do_up, do_down, first):
        if do_up:
            wa = wua_ref[0, 0]
            wl = wul_ref[0, 0]
        if do_down:
            wd = wd_ref[0]
            tail_a = car_a[f - 1]
            tail_l = car_l[f - 1]
        for c in range(n_chunks):
            rows = slice(c * rc, (c + 1) * rc)
            if do_down:
                ua = ua_buf[rows, :]
                ul = ul_buf[rows, :]
                ca = conv(jnp.concatenate([tail_a, ua], axis=0), cwa_ref, cba_ref)
                cl = conv(jnp.concatenate([tail_l, ul], axis=0), cwl_ref, cbl_ref)
                p = ((ca * _sigmoid(ca)) * cl).astype(BF16)
                tail_a = ua[rc - halo:, :]
                tail_l = ul[rc - halo:, :]
            if do_up:
                h = h_ref[rows, :]
                ua_buf[rows, :] = jnp.dot(h, wa, preferred_element_type=F32)
                ul_buf[rows, :] = jnp.dot(h, wl, preferred_element_type=F32)
            if do_down:
                y = jnp.dot(p, wd, preferred_element_type=F32)
                if first:
                    o_ref[0, rows, :] = y
                else:
                    o_ref[0, rows, :] += y
        if do_down:
            car_a[f - 1] = tail_a
            car_l[f - 1] = tail_l

    @pl.when(f == 0)
    def _():
        h_ref[...] = _norm_mod(x_ref[0], g_ref[0], mod_ref[0, 4:5, :], mod_ref[0, 3:4, :]).astype(BF16)
        run(True, False, False)

    @pl.when(jnp.logical_and(f > 0, t == 0))
    def _():
        car_a[f - 1] = jnp.zeros(car_a.shape[1:], F32)
        car_l[f - 1] = jnp.zeros(car_l.shape[1:], F32)

    @pl.when(f == 1)
    def _():
        run(True, True, True)

    @pl.when(jnp.logical_and(f > 1, f < nf))
    def _():
        run(True, True, False)

    @pl.when(f == nf)
    def _():
        run(False, True, False)
        y = x_ref[0] + mod_ref[0, 5:6, :] * o_ref[0]
        if final:
            ms = jnp.mean(y * y, axis=-1, keepdims=True)
            y = (y * lax.rsqrt(ms + EPS)) * fg_ref[...]
        o_ref[0] = y


def _ffn_call(x, mod, norm_g, w_up, ffn_conv_w, ffn_conv_b, w_down, final_g, layer, *, ts, tf, rc, final):
    bsz, seq, d = x.shape
    n_layers, d_ff, _ = w_down.shape
    kw = ffn_conv_w.shape[1]
    nf = d_ff // tf
    cb = ffn_conv_b.reshape(n_layers, 1, 2 * d_ff)
    up = lambda f: jnp.minimum(f, nf - 1)
    dn = lambda f: jnp.maximum(f - 1, 0)
    kern = functools.partial(_ffn_kernel, ts=ts, rc=rc, nf=nf, conv_w=kw, final=final)
    return pl.pallas_call(
        kern,
        grid=(bsz, seq // ts, nf + 1),
        in_specs=[
            pl.BlockSpec((1, ts, d), lambda b, t, f: (b, t, 0), pipeline_mode=pl.Buffered(1)),
            pl.BlockSpec((1, N_ADA, d), lambda b, t, f: (b, 0, 0)),
            pl.BlockSpec((1, 1, d), lambda b, t, f: (layer, 0, 0)),
            pl.BlockSpec((1, 1, d, tf), lambda b, t, f: (layer, up(f), 0, 0)),
            pl.BlockSpec((1, 1, d, tf), lambda b, t, f: (layer, nf + up(f), 0, 0)),
            pl.BlockSpec((1, kw, tf), lambda b, t, f: (layer, 0, dn(f))),
            pl.BlockSpec((1, kw, tf), lambda b, t, f: (layer, 0, nf + dn(f))),
            pl.BlockSpec((1, 1, tf), lambda b, t, f: (layer, 0, dn(f))),
            pl.BlockSpec((1, 1, tf), lambda b, t, f: (layer, 0, nf + dn(f))),
            pl.BlockSpec((1, tf, d), lambda b, t, f: (layer, dn(f), 0)),
            pl.BlockSpec((1, d), lambda b, t, f: (0, 0)),
        ],
        out_specs=pl.BlockSpec((1, ts, d), lambda b, t, f: (b, t, 0)),
        out_shape=jax.ShapeDtypeStruct((bsz, seq, d), F32),
        scratch_shapes=[
            pltpu.VMEM((ts, d), BF16),
            pltpu.VMEM((ts, tf), F32),
            pltpu.VMEM((ts, tf), F32),
            pltpu.VMEM((nf, SUBLANES, tf), F32),
            pltpu.VMEM((nf, SUBLANES, tf), F32),
        ],
        compiler_params=pltpu.CompilerParams(
            dimension_semantics=("arbitrary", "arbitrary", "arbitrary"), vmem_limit_bytes=VMEM_LIMIT),
        name="conv_ffn",
    )(x, mod, norm_g.reshape(n_layers, 1, d), w_up, w_up, ffn_conv_w, ffn_conv_w, cb, cb,
      w_down, final_g.reshape(1, d))


def kernel(x, c, ada_w, ada_b, norm_mix_g, w_in, b_in, conv_w, conv_b, lru_wa, lru_ba, lru_wx, lru_bx, lru_lambda, pool_w, pool_b, pool_scale, proj_a, proj_b, w_out, norm_ffn_g, w_up, ffn_conv_w, ffn_conv_b, w_down, final_g):
    bsz, seq, d = x.shape
    n_layers = ada_w.shape[0]
    d_rnn = conv_w.shape[-1]
    d_pool = pool_b.shape[-1]
    assert lru_wa.shape[1] == LRU_HEADS and pool_w.shape[1] == len(POOL_WINDOWS)
    assert ada_w.shape[-1] == N_ADA * d

    c_pad = jnp.pad(c, ((0, SUBLANES - bsz), (0, 0)))
    mod_all = _ada_call(c_pad, ada_w, ada_b)[:, :bsz].reshape(n_layers, bsz, N_ADA, d)

    ts, tn, bn, tf = 1024, 1024, 512, 512
    w_up_b = _col_blocks(w_up, tf)
    w_down_b = w_down.astype(BF16)
    for l in range(n_layers):
        mod = mod_all[l]
        zx, zg = _inproj_call(x, mod, norm_mix_g, w_in, b_in, l, d_rnn=d_rnn, d_pool=d_pool, ts=ts, tn=tn)
        ya = _rglru_call(zx, zg, conv_w, conv_b, lru_wa, lru_ba, lru_wx, lru_bx, lru_lambda, l,
                         d_rnn=d_rnn, ts=ts)
        yb = _pool_call(zx, pool_w, pool_b, pool_scale, l, d_rnn=d_rnn, d_pool=d_pool, ts=ts)
        x = _merge_call(ya, yb, zg, proj_a, proj_b, w_out, x, mod, l, ts=ts, bn=bn)
        x = _ffn_call(x, mod, norm_ffn_g, w_up_b, ffn_conv_w, ffn_conv_b, w_down_b, final_g, l,
                      ts=ts, tf=tf, rc=256, final=(l == n_layers - 1))
    return x
```

```python
import functools

import jax
import jax.numpy as jnp
from jax import lax
from jax.experimental import pallas as pl
from jax.experimental.pallas import tpu as pltpu

F32 = jnp.float32
BF16 = jnp.bfloat16

LRU_HEADS = 8
LRU_C = 8.0
POOL_WINDOWS = (2, 4, 8, 16)
N_ADA = 6
EPS = 1e-6
GELU_C0 = 0.7978845608028654
GELU_C1 = 0.044715

SUBLANES = 8
POOL_HALO = 16
VMEM_LIMIT = 56 * 1024 * 1024


def _sigmoid(x):
    return 0.5 * (1.0 + jnp.tanh(0.5 * x))


def _gelu_tanh(x):
    return 0.5 * x * (1.0 + jnp.tanh(GELU_C0 * (x + GELU_C1 * (x * x * x))))


def _log_sigmoid(x):
    return jnp.minimum(x, 0.0) - jnp.log1p(jnp.exp(-jnp.abs(x)))


def _norm_mod(x, g, scale, shift):
    ms = jnp.mean(x * x, axis=-1, keepdims=True)
    y = (x * lax.rsqrt(ms + EPS)) * g
    return y * (1.0 + scale) + shift


def _cast_kernel(w_ref, o_ref):
    o_ref[...] = w_ref[...].astype(BF16).reshape(o_ref.shape)


def _col_blocks(w, tn):
    n_layers, k, n = w.shape
    return pl.pallas_call(
        _cast_kernel,
        grid=(n_layers, n // tn),
        in_specs=[pl.BlockSpec((1, k, tn), lambda l, j: (l, 0, j))],
        out_specs=pl.BlockSpec((1, 1, k, tn), lambda l, j: (l, j, 0, 0)),
        out_shape=jax.ShapeDtypeStruct((n_layers, n // tn, k, tn), BF16),
        compiler_params=pltpu.CompilerParams(dimension_semantics=("arbitrary", "arbitrary")),
        name="cast_cols",
    )(w)


def _row_blocks(w, tk):
    n_layers, k, n = w.shape
    return pl.pallas_call(
        _cast_kernel,
        grid=(n_layers, k // tk),
        in_specs=[pl.BlockSpec((1, tk, n), lambda l, j: (l, j, 0))],
        out_specs=pl.BlockSpec((1, tk, n), lambda l, j: (l, j, 0)),
        out_shape=jax.ShapeDtypeStruct((n_layers, k, n), BF16),
        compiler_params=pltpu.CompilerParams(dimension_semantics=("arbitrary", "arbitrary")),
        name="cast_rows",
    )(w)


def _shift_rows(v, k):
    return pltpu.roll(v, k, 0)


def _ada_kernel(c_ref, w_ref, b_ref, o_ref):
    c = c_ref[...]
    ca = (c * _sigmoid(c)).astype(BF16)
    o_ref[0] = jnp.dot(ca, w_ref[0].astype(BF16), preferred_element_type=F32) + b_ref[0]


def _ada_call(c_pad, ada_w, ada_b):
    n_layers, d, n_out = ada_w.shape
    tn = 2048
    rows = c_pad.shape[0]
    return pl.pallas_call(
        _ada_kernel,
        grid=(n_layers, n_out // tn),
        in_specs=[
            pl.BlockSpec((rows, d), lambda l, n: (0, 0)),
            pl.BlockSpec((1, d, tn), lambda l, n: (l, 0, n)),
            pl.BlockSpec((1, 1, tn), lambda l, n: (l, 0, n)),
        ],
        out_specs=pl.BlockSpec((1, rows, tn), lambda l, n: (l, 0, n)),
        out_shape=jax.ShapeDtypeStruct((n_layers, rows, n_out), F32),
        compiler_params=pltpu.CompilerParams(
            dimension_semantics=("arbitrary", "arbitrary"), vmem_limit_bytes=VMEM_LIMIT),
        name="ada_mod",
    )(c_pad, ada_w, ada_b.reshape(n_layers, 1, n_out))


def _inproj_kernel(x_ref, mod_ref, g_ref, w_ref, b_ref, zx_ref, zg_ref, h_ref, *, n_x, n_gelu, rc):
    n = pl.program_id(2)

    @pl.when(n == 0)
    def _():
        h_ref[...] = _norm_mod(x_ref[0], g_ref[0], mod_ref[0, 1:2, :], mod_ref[0, 0:1, :]).astype(BF16)

    def z_rows(w, rows):
        return jnp.dot(h_ref[rows, :], w, preferred_element_type=F32) + b_ref[0]

    ts = h_ref.shape[0]
    chunks = [slice(r, r + rc) for r in range(0, ts, rc)]

    @pl.when(n < n_x)
    def _():
        w = w_ref[0].astype(BF16)
        for rows in chunks:
            zx_ref[0, rows, :] = z_rows(w, rows)

    @pl.when(n >= n_x)
    def _():
        w = w_ref[0].astype(BF16)
        is_gelu = n < n_x + n_gelu
        for rows in chunks:
            z = z_rows(w, rows)
            arg = jnp.where(is_gelu, GELU_C0 * (z + GELU_C1 * (z * z * z)), 0.5 * z)
            lead = jnp.where(is_gelu, 0.5 * z, 0.5)
            zg_ref[0, rows, :] = (lead * (1.0 + jnp.tanh(arg))).astype(BF16)


def _inproj_call(x, mod, norm_g, w_in, b_in, layer, *, d_rnn, d_pool, ts, tn, rc):
    bsz, seq, d = x.shape
    n_layers, d_in = b_in.shape
    d_gates = d_in - 2 * d_rnn - d_pool
    nb_rnn, nb_pool, nb_gates = d_rnn // tn, d_pool // tn, d_gates // tn
    n_x = nb_rnn + nb_pool
    n_steps = n_x + nb_rnn + nb_gates

    def w_col(n):
        return jnp.where(n < nb_rnn, n,
                         jnp.where(n < n_x, n + nb_rnn,
                                   jnp.where(n < n_x + nb_rnn, n - nb_pool, n)))

    kern = functools.partial(_inproj_kernel, n_x=n_x, n_gelu=nb_rnn, rc=rc)
    return pl.pallas_call(
        kern,
        grid=(bsz, seq // ts, n_steps),
        in_specs=[
            pl.BlockSpec((1, ts, d), lambda b, t, n: (b, t, 0), pipeline_mode=pl.Buffered(1)),
            pl.BlockSpec((1, N_ADA, d), lambda b, t, n: (b, 0, 0)),
            pl.BlockSpec((1, 1, d), lambda b, t, n: (layer, 0, 0)),
            pl.BlockSpec((1, d, tn), lambda b, t, n: (layer, 0, w_col(n))),
            pl.BlockSpec((1, 1, tn), lambda b, t, n: (layer, 0, w_col(n))),
        ],
        out_specs=[
            pl.BlockSpec((1, ts, tn), lambda b, t, n: (b, t, jnp.minimum(n, n_x - 1))),
            pl.BlockSpec((1, ts, tn), lambda b, t, n: (b, t, jnp.maximum(n - n_x, 0))),
        ],
        out_shape=[
            jax.ShapeDtypeStruct((bsz, seq, d_rnn + d_pool), F32),
            jax.ShapeDtypeStruct((bsz, seq, d_rnn + d_gates), BF16),
        ],
        scratch_shapes=[pltpu.VMEM((ts, d), BF16)],
        compiler_params=pltpu.CompilerParams(
            dimension_semantics=("arbitrary", "arbitrary", "arbitrary"), vmem_limit_bytes=VMEM_LIMIT),
        name="in_proj",
    )(x, mod, norm_g.reshape(n_layers, 1, d), w_in, b_in.reshape(n_layers, 1, d_in))


def _rglru_kernel(xr_ref, gg_ref, cw_ref, cb_ref, wa_ref, ba_ref, wx_ref, bx_ref, lam_ref,
                  ya_ref, xbuf, a_buf, b_buf, h_buf, hcar, *, ts, conv_w):
    t = pl.program_id(2)
    halo = SUBLANES

    @pl.when(t == 0)
    def _():
        xbuf[0:halo, :] = jnp.zeros((halo, xbuf.shape[1]), F32)
        hcar[...] = jnp.zeros(hcar.shape, F32)

    xbuf[halo:halo + ts, :] = xr_ref[0]
    v = xbuf[...]
    cw = cw_ref[0]
    acc = cw[conv_w - 1:conv_w, :] * v
    for k in range(1, conv_w):
        acc = acc + cw[conv_w - 1 - k:conv_w - k, :] * _shift_rows(v, k)
    xr = acc[halo:, :] + cb_ref[0]
    xbuf[0:halo, :] = v[ts:ts + halo, :]

    xrb = xr.astype(BF16)
    r = _sigmoid(jnp.dot(xrb, wa_ref[0, 0].astype(BF16), preferred_element_type=F32) + ba_ref[0])
    i = _sigmoid(jnp.dot(xrb, wx_ref[0, 0].astype(BF16), preferred_element_type=F32) + bx_ref[0])
    log_a = (LRU_C * r) * _log_sigmoid(lam_ref[0])
    a = jnp.exp(log_a)
    b = jnp.sqrt(1.0 - a * a) * (i * xr)

    grouped = (ts // SUBLANES, SUBLANES, a.shape[-1])
    a = a.reshape(grouped)
    b = b.reshape(grouped)
    rowm = lax.broadcasted_iota(jnp.int32, grouped, 1)
    s = 1
    while s < SUBLANES:
        keep = rowm >= s
        a_sh = jnp.where(keep, pltpu.roll(a, s, 1), 1.0)
        b_sh = jnp.where(keep, pltpu.roll(b, s, 1), 0.0)
        b = a * b_sh + b
        a = a * a_sh
        s *= 2
    a_buf[...] = a.reshape(ts, grouped[-1])
    b_buf[...] = b.reshape(ts, grouped[-1])

    def body(g, hc):
        r0 = pl.multiple_of(g * SUBLANES, SUBLANES)
        h8 = a_buf[pl.ds(r0, SUBLANES), :] * hc + b_buf[pl.ds(r0, SUBLANES), :]
        h_buf[pl.ds(r0, SUBLANES), :] = h8
        return jnp.broadcast_to(h8[SUBLANES - 1:SUBLANES, :], hc.shape)

    hcar[...] = lax.fori_loop(0, ts // SUBLANES, body, hcar[...], unroll=8)
    ya_ref[0] = (h_buf[...] * gg_ref[0].astype(F32)).astype(BF16)


def _rglru_call(zx, zg, conv_w, conv_b, lru_wa, lru_ba, lru_wx, lru_bx, lru_lambda, layer, *, d_rnn, ts):
    bsz, seq, _ = zx.shape
    n_layers, kw, _ = conv_w.shape
    heads, wblk = lru_wa.shape[1], lru_wa.shape[2]
    vec = lambda a: a.reshape(n_layers, 1, d_rnn)
    vspec = pl.BlockSpec((1, 1, wblk), lambda b, j, t: (layer, 0, j))
    mspec = pl.BlockSpec((1, 1, wblk, wblk), lambda b, j, t: (layer, j, 0, 0))
    kern = functools.partial(_rglru_kernel, ts=ts, conv_w=kw)
    return pl.pallas_call(
        kern,
        grid=(bsz, heads, seq // ts),
        in_specs=[
            pl.BlockSpec((1, ts, wblk), lambda b, j, t: (b, t, j)),
            pl.BlockSpec((1, ts, wblk), lambda b, j, t: (b, t, j)),
            pl.BlockSpec((1, kw, wblk), lambda b, j, t: (layer, 0, j)),
            vspec, mspec, vspec, mspec, vspec, vspec,
        ],
        out_specs=pl.BlockSpec((1, ts, wblk), lambda b, j, t: (b, t, j)),
        out_shape=jax.ShapeDtypeStruct((bsz, seq, d_rnn), BF16),
        scratch_shapes=[
            pltpu.VMEM((SUBLANES + ts, wblk), F32),
            pltpu.VMEM((ts, wblk), F32),
            pltpu.VMEM((ts, wblk), F32),
            pltpu.VMEM((ts, wblk), F32),
            pltpu.VMEM((SUBLANES, wblk), F32),
        ],
        compiler_params=pltpu.CompilerParams(
            dimension_semantics=("arbitrary", "arbitrary", "arbitrary"), vmem_limit_bytes=VMEM_LIMIT),
        name="rg_lru",
    )(zx, zg, conv_w, vec(conv_b), lru_wa, vec(lru_ba), lru_wx, vec(lru_bx), vec(lru_lambda))


def _pool_kernel(xp_ref, w_ref, b_ref, s_ref, yb_ref, xbuf, *, ts, gw):
    t = pl.program_id(1)
    halo = POOL_HALO

    @pl.when(t == 0)
    def _():
        xbuf[0:halo, :] = jnp.zeros((halo, xbuf.shape[1]), F32)

    xbuf[halo:halo + ts, :] = xp_ref[0]
    pos = (t * ts + lax.broadcasted_iota(jnp.int32, (ts, gw), 0)).astype(F32)
    for g, win in enumerate(POOL_WINDOWS):
        v = xbuf[:, g * gw:(g + 1) * gw]
        sm = v
        k = 1
        while k < win:
            sm = sm + _shift_rows(sm, k)
            k *= 2
        cur = v[halo:, :]
        count = jnp.minimum(pos + 1.0, float(win))
        p = (sm[halo:, :] / count - cur).astype(BF16)
        y = jnp.dot(p, w_ref[0, g].astype(BF16), preferred_element_type=F32) + b_ref[0, :, g * gw:(g + 1) * gw]
        yb_ref[0, :, g * gw:(g + 1) * gw] = (y * s_ref[0, :, g * gw:(g + 1) * gw]).astype(BF16)
    xbuf[0:halo, :] = xbuf[ts:ts + halo, :]


def _pool_call(zx, pool_w, pool_b, pool_scale, layer, *, d_rnn, d_pool, ts):
    bsz, seq, _ = zx.shape
    n_layers, groups, gw, _ = pool_w.shape
    col0 = d_rnn // d_pool
    vec = lambda a: a.reshape(n_layers, 1, d_pool)
    vspec = pl.BlockSpec((1, 1, d_pool), lambda b, t: (layer, 0, 0))
    kern = functools.partial(_pool_kernel, ts=ts, gw=gw)
    return pl.pallas_call(
        kern,
        grid=(bsz, seq // ts),
        in_specs=[
            pl.BlockSpec((1, ts, d_pool), lambda b, t: (b, t, col0)),
            pl.BlockSpec((1, groups, gw, gw), lambda b, t: (layer, 0, 0, 0)),
            vspec, vspec,
        ],
        out_specs=pl.BlockSpec((1, ts, d_pool), lambda b, t: (b, t, 0)),
        out_shape=jax.ShapeDtypeStruct((bsz, seq, d_pool), BF16),
        scratch_shapes=[pltpu.VMEM((POOL_HALO + ts, d_pool), F32)],
        compiler_params=pltpu.CompilerParams(
            dimension_semantics=("arbitrary", "arbitrary"), vmem_limit_bytes=VMEM_LIMIT),
        name="ms_pool",
    )(zx, pool_w, vec(pool_b), vec(pool_scale))


def _merge_kernel(ya_ref, yb_ref, ga_ref, gb_ref, pa_ref, pb_ref, wo_ref, x_ref, mod_ref,
                  o_ref, m_ref, *, nb, bn):
    n = pl.program_id(2)

    @pl.when(n < nb)
    def _():
        pa = jnp.dot(ya_ref[0], pa_ref[0].astype(BF16), preferred_element_type=F32)
        pb = jnp.dot(yb_ref[0], pb_ref[0].astype(BF16), preferred_element_type=F32)
        merged = ga_ref[0].astype(F32) * pa + gb_ref[0].astype(F32) * pb
        c0 = pl.multiple_of(n * bn, bn)
        m_ref[:, pl.ds(c0, bn)] = merged.astype(BF16)

    @pl.when(n >= nb)
    def _():
        y = jnp.dot(m_ref[...], wo_ref[0].astype(BF16), preferred_element_type=F32)
        o_ref[0] = x_ref[0] + mod_ref[0, 2:3, :] * y


def _merge_call(ya, yb, zg, proj_a, proj_b, w_out, x, mod, layer, *, ts, bn):
    bsz, seq, d = x.shape
    d_rnn, d_pool = ya.shape[-1], yb.shape[-1]
    nb = d // bn
    ga0 = d_rnn // bn
    gb0 = ga0 + nb
    first = lambda n: jnp.minimum(n, nb - 1)
    second = lambda n: jnp.maximum(n - nb, 0)
    kern = functools.partial(_merge_kernel, nb=nb, bn=bn)
    return pl.pallas_call(
        kern,
        grid=(bsz, seq // ts, 2 * nb),
        in_specs=[
            pl.BlockSpec((1, ts, d_rnn), lambda b, t, n: (b, t, 0)),
            pl.BlockSpec((1, ts, d_pool), lambda b, t, n: (b, t, 0)),
            pl.BlockSpec((1, ts, bn), lambda b, t, n: (b, t, ga0 + first(n))),
            pl.BlockSpec((1, ts, bn), lambda b, t, n: (b, t, gb0 + first(n))),
            pl.BlockSpec((1, d_rnn, bn), lambda b, t, n: (layer, 0, first(n))),
            pl.BlockSpec((1, d_pool, bn), lambda b, t, n: (layer, 0, first(n))),
            pl.BlockSpec((1, d, bn), lambda b, t, n: (layer, 0, second(n))),
            pl.BlockSpec((1, ts, bn), lambda b, t, n: (b, t, second(n))),
            pl.BlockSpec((1, N_ADA, bn), lambda b, t, n: (b, 0, second(n))),
        ],
        out_specs=pl.BlockSpec((1, ts, bn), lambda b, t, n: (b, t, second(n))),
        out_shape=jax.ShapeDtypeStruct((bsz, seq, d), F32),
        scratch_shapes=[pltpu.VMEM((ts, d), BF16)],
        compiler_params=pltpu.CompilerParams(
            dimension_semantics=("arbitrary", "arbitrary", "arbitrary"), vmem_limit_bytes=VMEM_LIMIT),
        name="merge",
    )(ya, yb, zg, zg, proj_a, proj_b, w_out, x, mod)


def _ffn_kernel(x_ref, mod_ref, g_ref, wua_ref, wul_ref, cwa_ref, cwl_ref, cba_ref, cbl_ref,
                wd_ref, fg_ref, o_ref, h_ref, ua_buf, ul_buf, car_a, car_l, *, ts, rc, nf, conv_w, final):
    t = pl.program_id(1)
    f = pl.program_id(2)
    halo = SUBLANES
    n_chunks = ts // rc

    def conv(v, cw_ref, cb_ref):
        cw = cw_ref[0]
        acc = cw[conv_w - 1:conv_w, :] * v
        for k in range(1, conv_w):
            acc = acc + cw[conv_w - 1 - k:conv_w - k, :] * _shift_rows(v, k)
        return acc[halo:, :] + cb_ref[0]

    def run(do_up, do_down, first):
        if do_up:
            wa = wua_ref[0, 0]
            wl = wul_ref[0, 0]
        if do_down:
            wd = wd_ref[0]
            tail_a = car_a[f - 1]
            tail_l = car_l[f - 1]
        for c in range(n_chunks):
            rows = slice(c * rc, (c + 1) * rc)
            if do_down:
                ua = ua_buf[rows, :]
                ul = ul_buf[rows, :]
                ca = conv(jnp.concatenate([tail_a, ua], axis=0), cwa_ref, cba_ref)
                cl = conv(jnp.concatenate([tail_l, ul], axis=0), cwl_ref, cbl_ref)
                p = ((ca * _sigmoid(ca)) * cl).astype(BF16)
                tail_a = ua[rc - halo:, :]
                tail_l = ul[rc - halo:, :]
            if do_up:
                h = h_ref[rows, :]
                ua_buf[rows, :] = jnp.dot(h, wa, preferred_element_type=F32)
                ul_buf[rows, :] = jnp.dot(h, wl, preferred_element_type=F32)
            if do_down:
                y = jnp.dot(p, wd, preferred_element_type=F32)
                if first:
                    o_ref[0, rows, :] = y
                else:
                    o_ref[0, rows, :] += y
        if do_down:
            car_a[f - 1] = tail_a
            car_l[f - 1] = tail_l

    @pl.when(f == 0)
    def _():
        h_ref[...] = _norm_mod(x_ref[0], g_ref[0], mod_ref[0, 4:5, :], mod_ref[0, 3:4, :]).astype(BF16)
        run(True, False, False)

    @pl.when(jnp.logical_and(f > 0, t == 0))
    def _():
        car_a[f - 1] = jnp.zeros(car_a.shape[1:], F32)
        car_l[f - 1] = jnp.zeros(car_l.shape[1:], F32)

    @pl.when(f == 1)
    def _():
        run(True, True, True)

    @pl.when(jnp.logical_and(f > 1, f < nf))
    def _():
        run(True, True, False)

    @pl.when(f == nf)
    def _():
        run(False, True, False)
        y = x_ref[0] + mod_ref[0, 5:6, :] * o_ref[0]
        if final:
            ms = jnp.mean(y * y, axis=-1, keepdims=True)
            y = (y * lax.rsqrt(ms + EPS)) * fg_ref[...]
        o_ref[0] = y


def _ffn_call(x, mod, norm_g, w_up, ffn_conv_w, ffn_conv_b, w_down, final_g, layer, *, ts, tf, rc, final):
    bsz, seq, d = x.shape
    n_layers, d_ff, _ = w_down.shape
    kw = ffn_conv_w.shape[1]
    nf = d_ff // tf
    cb = ffn_conv_b.reshape(n_layers, 1, 2 * d_ff)
    up = lambda f: jnp.minimum(f, nf - 1)
    dn = lambda f: jnp.maximum(f - 1, 0)
    kern = functools.partial(_ffn_kernel, ts=ts, rc=rc, nf=nf, conv_w=kw, final=final)
    return pl.pallas_call(
        kern,
        grid=(bsz, seq // ts, nf + 1),
        in_specs=[
            pl.BlockSpec((1, ts, d), lambda b, t, f: (b, t, 0), pipeline_mode=pl.Buffered(1)),
            pl.BlockSpec((1, N_ADA, d), lambda b, t, f: (b, 0, 0)),
            pl.BlockSpec((1, 1, d), lambda b, t, f: (layer, 0, 0)),
            pl.BlockSpec((1, 1, d, tf), lambda b, t, f: (layer, up(f), 0, 0)),
            pl.BlockSpec((1, 1, d, tf), lambda b, t, f: (layer, nf + up(f), 0, 0)),
            pl.BlockSpec((1, kw, tf), lambda b, t, f: (layer, 0, dn(f))),
            pl.BlockSpec((1, kw, tf), lambda b, t, f: (layer, 0, nf + dn(f))),
            pl.BlockSpec((1, 1, tf), lambda b, t, f: (layer, 0, dn(f))),
            pl.BlockSpec((1, 1, tf), lambda b, t, f: (layer, 0, nf + dn(f))),
            pl.BlockSpec((1, tf, d), lambda b, t, f: (layer, dn(f), 0)),
            pl.BlockSpec((1, d), lambda b, t, f: (0, 0)),
        ],
        out_specs=pl.BlockSpec((1, ts, d), lambda b, t, f: (b, t, 0)),
        out_shape=jax.ShapeDtypeStruct((bsz, seq, d), F32),
        scratch_shapes=[
            pltpu.VMEM((ts, d), BF16),
            pltpu.VMEM((ts, tf), F32),
            pltpu.VMEM((ts, tf), F32),
            pltpu.VMEM((nf, SUBLANES, tf), F32),
            pltpu.VMEM((nf, SUBLANES, tf), F32),
        ],
        compiler_params=pltpu.CompilerParams(
            dimension_semantics=("arbitrary", "arbitrary", "arbitrary"), vmem_limit_bytes=VMEM_LIMIT),
        name="conv_ffn",
    )(x, mod, norm_g.reshape(n_layers, 1, d), w_up, w_up, ffn_conv_w, ffn_conv_w, cb, cb,
      w_down, final_g.reshape(1, d))


def kernel(x, c, ada_w, ada_b, norm_mix_g, w_in, b_in, conv_w, conv_b, lru_wa, lru_ba, lru_wx, lru_bx, lru_lambda, pool_w, pool_b, pool_scale, proj_a, proj_b, w_out, norm_ffn_g, w_up, ffn_conv_w, ffn_conv_b, w_down, final_g):
    bsz, seq, d = x.shape
    n_layers = ada_w.shape[0]
    d_rnn = conv_w.shape[-1]
    d_pool = pool_b.shape[-1]
    assert lru_wa.shape[1] == LRU_HEADS and pool_w.shape[1] == len(POOL_WINDOWS)
    assert ada_w.shape[-1] == N_ADA * d

    c_pad = jnp.pad(c, ((0, SUBLANES - bsz), (0, 0)))
    mod_all = _ada_call(c_pad, ada_w, ada_b)[:, :bsz].reshape(n_layers, bsz, N_ADA, d)

    ts, tn, bn, tf = 1024, 1024, 512, 512
    w_up_b = _col_blocks(w_up, tf)
    w_down_b = _row_blocks(w_down, tf)
    for l in range(n_layers):
        mod = mod_all[l]
        zx, zg = _inproj_call(x, mod, norm_mix_g, w_in, b_in, l, d_rnn=d_rnn, d_pool=d_pool, ts=ts, tn=tn, rc=256)
        ya = _rglru_call(zx, zg, conv_w, conv_b, lru_wa, lru_ba, lru_wx, lru_bx, lru_lambda, l,
                         d_rnn=d_rnn, ts=ts)
        yb = _pool_call(zx, pool_w, pool_b, pool_scale, l, d_rnn=d_rnn, d_pool=d_pool, ts=ts)
        x = _merge_call(ya, yb, zg, proj_a, proj_b, w_out, x, mod, l, ts=ts, bn=bn)
        x = _ffn_call(x, mod, norm_ffn_g, w_up_b, ffn_conv_w, ffn_conv_b, w_down_b, final_g, l,
                      ts=ts, tf=tf, rc=256, final=(l == n_layers - 1))
    return x
```

```python
import functools

import jax
import jax.numpy as jnp
from jax import lax
from jax.experimental import pallas as pl
from jax.experimental.pallas import tpu as pltpu

F32 = jnp.float32
BF16 = jnp.bfloat16

LRU_HEADS = 8
LRU_C = 8.0
POOL_WINDOWS = (2, 4, 8, 16)
N_ADA = 6
EPS = 1e-6
GELU_C0 = 0.7978845608028654
GELU_C1 = 0.044715

SUBLANES = 8
POOL_HALO = 16
VMEM_LIMIT = 56 * 1024 * 1024


def _sigmoid(x):
    return 0.5 * (1.0 + jnp.tanh(0.5 * x))


def _gelu_tanh(x):
    return 0.5 * x * (1.0 + jnp.tanh(GELU_C0 * (x + GELU_C1 * (x * x * x))))


def _log_sigmoid(x):
    return jnp.minimum(x, 0.0) - jnp.log1p(jnp.exp(-jnp.abs(x)))


def _norm_mod(x, g, scale, shift):
    ms = jnp.mean(x * x, axis=-1, keepdims=True)
    y = (x * lax.rsqrt(ms + EPS)) * g
    return y * (1.0 + scale) + shift


def _cast_kernel(w_ref, o_ref):
    o_ref[...] = w_ref[...].astype(BF16).reshape(o_ref.shape)


def _col_blocks(w, tn):
    n_layers, k, n = w.shape
    return pl.pallas_call(
        _cast_kernel,
        grid=(n_layers, n // tn),
        in_specs=[pl.BlockSpec((1, k, tn), lambda l, j: (l, 0, j))],
        out_specs=pl.BlockSpec((1, 1, k, tn), lambda l, j: (l, j, 0, 0)),
        out_shape=jax.ShapeDtypeStruct((n_layers, n // tn, k, tn), BF16),
        compiler_params=pltpu.CompilerParams(dimension_semantics=("arbitrary", "arbitrary")),
        name="cast_cols",
    )(w)


def _row_blocks(w, tk):
    n_layers, k, n = w.shape
    return pl.pallas_call(
        _cast_kernel,
        grid=(n_layers, k // tk),
        in_specs=[pl.BlockSpec((1, tk, n), lambda l, j: (l, j, 0))],
        out_specs=pl.BlockSpec((1, tk, n), lambda l, j: (l, j, 0)),
        out_shape=jax.ShapeDtypeStruct((n_layers, k, n), BF16),
        compiler_params=pltpu.CompilerParams(dimension_semantics=("arbitrary", "arbitrary")),
        name="cast_rows",
    )(w)


def _shift_rows(v, k):
    return pltpu.roll(v, k, 0)


def _ada_kernel(c_ref, w_ref, b_ref, o_ref):
    c = c_ref[...]
    ca = (c * _sigmoid(c)).astype(BF16)
    o_ref[0] = jnp.dot(ca, w_ref[0].astype(BF16), preferred_element_type=F32) + b_ref[0]


def _ada_call(c_pad, ada_w, ada_b):
    n_layers, d, n_out = ada_w.shape
    tn = 2048
    rows = c_pad.shape[0]
    return pl.pallas_call(
        _ada_kernel,
        grid=(n_layers, n_out // tn),
        in_specs=[
            pl.BlockSpec((rows, d), lambda l, n: (0, 0)),
            pl.BlockSpec((1, d, tn), lambda l, n: (l, 0, n)),
            pl.BlockSpec((1, 1, tn), lambda l, n: (l, 0, n)),
        ],
        out_specs=pl.BlockSpec((1, rows, tn), lambda l, n: (l, 0, n)),
        out_shape=jax.ShapeDtypeStruct((n_layers, rows, n_out), F32),
        compiler_params=pltpu.CompilerParams(
            dimension_semantics=("arbitrary", "arbitrary"), vmem_limit_bytes=VMEM_LIMIT),
        name="ada_mod",
    )(c_pad, ada_w, ada_b.reshape(n_layers, 1, n_out))


def _inproj_kernel(x_ref, mod_ref, g_ref, w_ref, b_ref, zx_ref, zg_ref, h_ref, *, n_x, n_gelu, rc):
    n = pl.program_id(2)
    ts = h_ref.shape[0]
    chunks = [slice(r, r + rc) for r in range(0, ts, rc)]

    @pl.when(n == 0)
    def _():
        for rows in chunks:
            h_ref[rows, :] = _norm_mod(x_ref[0, rows, :], g_ref[0], mod_ref[0, 1:2, :],
                                       mod_ref[0, 0:1, :]).astype(BF16)

    def z_rows(w, rows):
        return jnp.dot(h_ref[rows, :], w, preferred_element_type=F32) + b_ref[0]


    @pl.when(n < n_x)
    def _():
        w = w_ref[0].astype(BF16)
        for rows in chunks:
            zx_ref[0, rows, :] = z_rows(w, rows)

    @pl.when(n >= n_x)
    def _():
        w = w_ref[0].astype(BF16)
        is_gelu = n < n_x + n_gelu
        for rows in chunks:
            z = z_rows(w, rows)
            arg = jnp.where(is_gelu, GELU_C0 * (z + GELU_C1 * (z * z * z)), 0.5 * z)
            lead = jnp.where(is_gelu, 0.5 * z, 0.5)
            zg_ref[0, rows, :] = (lead * (1.0 + jnp.tanh(arg))).astype(BF16)


def _inproj_call(x, mod, norm_g, w_in, b_in, layer, *, d_rnn, d_pool, ts, tn, rc):
    bsz, seq, d = x.shape
    n_layers, d_in = b_in.shape
    d_gates = d_in - 2 * d_rnn - d_pool
    nb_rnn, nb_pool, nb_gates = d_rnn // tn, d_pool // tn, d_gates // tn
    n_x = nb_rnn + nb_pool
    n_steps = n_x + nb_rnn + nb_gates

    def w_col(n):
        return jnp.where(n < nb_rnn, n,
                         jnp.where(n < n_x, n + nb_rnn,
                                   jnp.where(n < n_x + nb_rnn, n - nb_pool, n)))

    kern = functools.partial(_inproj_kernel, n_x=n_x, n_gelu=nb_rnn, rc=rc)
    return pl.pallas_call(
        kern,
        grid=(bsz, seq // ts, n_steps),
        in_specs=[
            pl.BlockSpec((1, ts, d), lambda b, t, n: (b, t, 0), pipeline_mode=pl.Buffered(1)),
            pl.BlockSpec((1, N_ADA, d), lambda b, t, n: (b, 0, 0)),
            pl.BlockSpec((1, 1, d), lambda b, t, n: (layer, 0, 0)),
            pl.BlockSpec((1, d, tn), lambda b, t, n: (layer, 0, w_col(n))),
            pl.BlockSpec((1, 1, tn), lambda b, t, n: (layer, 0, w_col(n))),
        ],
        out_specs=[
            pl.BlockSpec((1, ts, tn), lambda b, t, n: (b, t, jnp.minimum(n, n_x - 1))),
            pl.BlockSpec((1, ts, tn), lambda b, t, n: (b, t, jnp.maximum(n - n_x, 0))),
        ],
        out_shape=[
            jax.ShapeDtypeStruct((bsz, seq, d_rnn + d_pool), F32),
            jax.ShapeDtypeStruct((bsz, seq, d_rnn + d_gates), BF16),
        ],
        scratch_shapes=[pltpu.VMEM((ts, d), BF16)],
        compiler_params=pltpu.CompilerParams(
            dimension_semantics=("arbitrary", "arbitrary", "arbitrary"), vmem_limit_bytes=VMEM_LIMIT),
        name="in_proj",
    )(x, mod, norm_g.reshape(n_layers, 1, d), w_in, b_in.reshape(n_layers, 1, d_in))


def _rglru_kernel(xr_ref, gg_ref, cw_ref, cb_ref, wa_ref, ba_ref, wx_ref, bx_ref, lam_ref,
                  ya_ref, xbuf, a_buf, b_buf, h_buf, hcar, *, ts, conv_w):
    t = pl.program_id(2)
    halo = SUBLANES

    @pl.when(t == 0)
    def _():
        xbuf[0:halo, :] = jnp.zeros((halo, xbuf.shape[1]), F32)
        hcar[...] = jnp.zeros(hcar.shape, F32)

    xbuf[halo:halo + ts, :] = xr_ref[0]
    v = xbuf[...]
    cw = cw_ref[0]
    acc = cw[conv_w - 1:conv_w, :] * v
    for k in range(1, conv_w):
        acc = acc + cw[conv_w - 1 - k:conv_w - k, :] * _shift_rows(v, k)
    xr = acc[halo:, :] + cb_ref[0]
    xbuf[0:halo, :] = v[ts:ts + halo, :]

    xrb = xr.astype(BF16)
    r = _sigmoid(jnp.dot(xrb, wa_ref[0, 0].astype(BF16), preferred_element_type=F32) + ba_ref[0])
    i = _sigmoid(jnp.dot(xrb, wx_ref[0, 0].astype(BF16), preferred_element_type=F32) + bx_ref[0])
    log_a = (LRU_C * r) * _log_sigmoid(lam_ref[0])
    a = jnp.exp(log_a)
    b = jnp.sqrt(1.0 - a * a) * (i * xr)

    grouped = (ts // SUBLANES, SUBLANES, a.shape[-1])
    a = a.reshape(grouped)
    b = b.reshape(grouped)
    rowm = lax.broadcasted_iota(jnp.int32, grouped, 1)
    s = 1
    while s < SUBLANES:
        keep = rowm >= s
        a_sh = jnp.where(keep, pltpu.roll(a, s, 1), 1.0)
        b_sh = jnp.where(keep, pltpu.roll(b, s, 1), 0.0)
        b = a * b_sh + b
        a = a * a_sh
        s *= 2
    a_buf[...] = a.reshape(ts, grouped[-1])
    b_buf[...] = b.reshape(ts, grouped[-1])

    def body(g, hc):
        r0 = pl.multiple_of(g * SUBLANES, SUBLANES)
        h8 = a_buf[pl.ds(r0, SUBLANES), :] * hc + b_buf[pl.ds(r0, SUBLANES), :]
        h_buf[pl.ds(r0, SUBLANES), :] = h8
        return jnp.broadcast_to(h8[SUBLANES - 1:SUBLANES, :], hc.shape)

    hcar[...] = lax.fori_loop(0, ts // SUBLANES, body, hcar[...], unroll=8)
    ya_ref[0] = (h_buf[...] * gg_ref[0].astype(F32)).astype(BF16)


def _rglru_call(zx, zg, conv_w, conv_b, lru_wa, lru_ba, lru_wx, lru_bx, lru_lambda, layer, *, d_rnn, ts):
    bsz, seq, _ = zx.shape
    n_layers, kw, _ = conv_w.shape
    heads, wblk = lru_wa.shape[1], lru_wa.shape[2]
    vec = lambda a: a.reshape(n_layers, 1, d_rnn)
    vspec = pl.BlockSpec((1, 1, wblk), lambda b, j, t: (layer, 0, j))
    mspec = pl.BlockSpec((1, 1, wblk, wblk), lambda b, j, t: (layer, j, 0, 0))
    kern = functools.partial(_rglru_kernel, ts=ts, conv_w=kw)
    return pl.pallas_call(
        kern,
        grid=(bsz, heads, seq // ts),
        in_specs=[
            pl.BlockSpec((1, ts, wblk), lambda b, j, t: (b, t, j)),
            pl.BlockSpec((1, ts, wblk), lambda b, j, t: (b, t, j)),
            pl.BlockSpec((1, kw, wblk), lambda b, j, t: (layer, 0, j)),
            vspec, mspec, vspec, mspec, vspec, vspec,
        ],
        out_specs=pl.BlockSpec((1, ts, wblk), lambda b, j, t: (b, t, j)),
        out_shape=jax.ShapeDtypeStruct((bsz, seq, d_rnn), BF16),
        scratch_shapes=[
            pltpu.VMEM((SUBLANES + ts, wblk), F32),
            pltpu.VMEM((ts, wblk), F32),
            pltpu.VMEM((ts, wblk), F32),
            pltpu.VMEM((ts, wblk), F32),
            pltpu.VMEM((SUBLANES, wblk), F32),
        ],
        compiler_params=pltpu.CompilerParams(
            dimension_semantics=("arbitrary", "arbitrary", "arbitrary"), vmem_limit_bytes=VMEM_LIMIT),
        name="rg_lru",
    )(zx, zg, conv_w, vec(conv_b), lru_wa, vec(lru_ba), lru_wx, vec(lru_bx), vec(lru_lambda))


def _pool_kernel(xp_ref, w_ref, b_ref, s_ref, yb_ref, xbuf, *, ts, gw):
    t = pl.program_id(1)
    halo = POOL_HALO

    @pl.when(t == 0)
    def _():
        xbuf[0:halo, :] = jnp.zeros((halo, xbuf.shape[1]), F32)

    xbuf[halo:halo + ts, :] = xp_ref[0]
    pos = (t * ts + lax.broadcasted_iota(jnp.int32, (ts, gw), 0)).astype(F32)
    for g, win in enumerate(POOL_WINDOWS):
        v = xbuf[:, g * gw:(g + 1) * gw]
        sm = v
        k = 1
        while k < win:
            sm = sm + _shift_rows(sm, k)
            k *= 2
        cur = v[halo:, :]
        count = jnp.minimum(pos + 1.0, float(win))
        p = (sm[halo:, :] / count - cur).astype(BF16)
        y = jnp.dot(p, w_ref[0, g].astype(BF16), preferred_element_type=F32) + b_ref[0, :, g * gw:(g + 1) * gw]
        yb_ref[0, :, g * gw:(g + 1) * gw] = (y * s_ref[0, :, g * gw:(g + 1) * gw]).astype(BF16)
    xbuf[0:halo, :] = xbuf[ts:ts + halo, :]


def _pool_call(zx, pool_w, pool_b, pool_scale, layer, *, d_rnn, d_pool, ts):
    bsz, seq, _ = zx.shape
    n_layers, groups, gw, _ = pool_w.shape
    col0 = d_rnn // d_pool
    vec = lambda a: a.reshape(n_layers, 1, d_pool)
    vspec = pl.BlockSpec((1, 1, d_pool), lambda b, t: (layer, 0, 0))
    kern = functools.partial(_pool_kernel, ts=ts, gw=gw)
    return pl.pallas_call(
        kern,
        grid=(bsz, seq // ts),
        in_specs=[
            pl.BlockSpec((1, ts, d_pool), lambda b, t: (b, t, col0)),
            pl.BlockSpec((1, groups, gw, gw), lambda b, t: (layer, 0, 0, 0)),
            vspec, vspec,
        ],
        out_specs=pl.BlockSpec((1, ts, d_pool), lambda b, t: (b, t, 0)),
        out_shape=jax.ShapeDtypeStruct((bsz, seq, d_pool), BF16),
        scratch_shapes=[pltpu.VMEM((POOL_HALO + ts, d_pool), F32)],
        compiler_params=pltpu.CompilerParams(
            dimension_semantics=("arbitrary", "arbitrary"), vmem_limit_bytes=VMEM_LIMIT),
        name="ms_pool",
    )(zx, pool_w, vec(pool_b), vec(pool_scale))


def _merge_kernel(ya_ref, yb_ref, ga_ref, gb_ref, pa_ref, pb_ref, wo_ref, x_ref, mod_ref,
                  o_ref, m_ref, *, nb, bn):
    n = pl.program_id(2)

    @pl.when(n < nb)
    def _():
        pa = jnp.dot(ya_ref[0], pa_ref[0].astype(BF16), preferred_element_type=F32)
        pb = jnp.dot(yb_ref[0], pb_ref[0].astype(BF16), preferred_element_type=F32)
        merged = ga_ref[0].astype(F32) * pa + gb_ref[0].astype(F32) * pb
        c0 = pl.multiple_of(n * bn, bn)
        m_ref[:, pl.ds(c0, bn)] = merged.astype(BF16)

    @pl.when(n >= nb)
    def _():
        y = jnp.dot(m_ref[...], wo_ref[0].astype(BF16), preferred_element_type=F32)
        o_ref[0] = x_ref[0] + mod_ref[0, 2:3, :] * y


def _merge_call(ya, yb, zg, proj_a, proj_b, w_out, x, mod, layer, *, ts, bn):
    bsz, seq, d = x.shape
    d_rnn, d_pool = ya.shape[-1], yb.shape[-1]
    nb = d // bn
    ga0 = d_rnn // bn
    gb0 = ga0 + nb
    first = lambda n: jnp.minimum(n, nb - 1)
    second = lambda n: jnp.maximum(n - nb, 0)
    kern = functools.partial(_merge_kernel, nb=nb, bn=bn)
    return pl.pallas_call(
        kern,
        grid=(bsz, seq // ts, 2 * nb),
        in_specs=[
            pl.BlockSpec((1, ts, d_rnn), lambda b, t, n: (b, t, 0), pipeline_mode=pl.Buffered(1)),
            pl.BlockSpec((1, ts, d_pool), lambda b, t, n: (b, t, 0), pipeline_mode=pl.Buffered(1)),
            pl.BlockSpec((1, ts, bn), lambda b, t, n: (b, t, ga0 + first(n))),
            pl.BlockSpec((1, ts, bn), lambda b, t, n: (b, t, gb0 + first(n))),
            pl.BlockSpec((1, d_rnn, bn), lambda b, t, n: (layer, 0, first(n))),
            pl.BlockSpec((1, d_pool, bn), lambda b, t, n: (layer, 0, first(n))),
            pl.BlockSpec((1, d, bn), lambda b, t, n: (layer, 0, second(n))),
            pl.BlockSpec((1, ts, bn), lambda b, t, n: (b, t, second(n))),
            pl.BlockSpec((1, N_ADA, bn), lambda b, t, n: (b, 0, second(n))),
        ],
        out_specs=pl.BlockSpec((1, ts, bn), lambda b, t, n: (b, t, second(n))),
        out_shape=jax.ShapeDtypeStruct((bsz, seq, d), F32),
        scratch_shapes=[pltpu.VMEM((ts, d), BF16)],
        compiler_params=pltpu.CompilerParams(
            dimension_semantics=("arbitrary", "arbitrary", "arbitrary"), vmem_limit_bytes=VMEM_LIMIT),
        name="merge",
    )(ya, yb, zg, zg, proj_a, proj_b, w_out, x, mod)


def _ffn_kernel(x_ref, mod_ref, g_ref, wua_ref, wul_ref, cwa_ref, cwl_ref, cba_ref, cbl_ref,
                wd_ref, fg_ref, o_ref, h_ref, ua_buf, ul_buf, car_a, car_l, *, ts, rc, nf, conv_w, final):
    t = pl.program_id(1)
    f = pl.program_id(2)
    halo = SUBLANES
    n_chunks = ts // rc

    def conv(v, cw_ref, cb_ref):
        cw = cw_ref[0]
        acc = cw[conv_w - 1:conv_w, :] * v
        for k in range(1, conv_w):
            acc = acc + cw[conv_w - 1 - k:conv_w - k, :] * _shift_rows(v, k)
        return acc[halo:, :] + cb_ref[0]

    def run(do_up, do_down, first):
        if do_up:
            wa = wua_ref[0, 0]
            wl = wul_ref[0, 0]
        if do_down:
            wd = wd_ref[0]
            tail_a = car_a[f - 1]
            tail_l = car_l[f - 1]
        for c in range(n_chunks):
            rows = slice(c * rc, (c + 1) * rc)
            if do_down:
                ua = ua_buf[rows, :]
                ul = ul_buf[rows, :]
                ca = conv(jnp.concatenate([tail_a, ua], axis=0), cwa_ref, cba_ref)
                cl = conv(jnp.concatenate([tail_l, ul], axis=0), cwl_ref, cbl_ref)
                p = ((ca * _sigmoid(ca)) * cl).astype(BF16)
                tail_a = ua[rc - halo:, :]
                tail_l = ul[rc - halo:, :]
            if do_up:
                h = h_ref[rows, :]
                ua_buf[rows, :] = jnp.dot(h, wa, preferred_element_type=F32)
                ul_buf[rows, :] = jnp.dot(h, wl, preferred_element_type=F32)
            if do_down:
                y = jnp.dot(p, wd, preferred_element_type=F32)
                if first:
                    o_ref[0, rows, :] = y
                else:
                    o_ref[0, rows, :] += y
        if do_down:
            car_a[f - 1] = tail_a
            car_l[f - 1] = tail_l

    @pl.when(f == 0)
    def _():
        h_ref[...] = _norm_mod(x_ref[0], g_ref[0], mod_ref[0, 4:5, :], mod_ref[0, 3:4, :]).astype(BF16)
        run(True, False, False)

    @pl.when(jnp.logical_and(f > 0, t == 0))
    def _():
        car_a[f - 1] = jnp.zeros(car_a.shape[1:], F32)
        car_l[f - 1] = jnp.zeros(car_l.shape[1:], F32)

    @pl.when(f == 1)
    def _():
        run(True, True, True)

    @pl.when(jnp.logical_and(f > 1, f < nf))
    def _():
        run(True, True, False)

    @pl.when(f == nf)
    def _():
        run(False, True, False)
        y = x_ref[0] + mod_ref[0, 5:6, :] * o_ref[0]
        if final:
            ms = jnp.mean(y * y, axis=-1, keepdims=True)
            y = (y * lax.rsqrt(ms + EPS)) * fg_ref[...]
        o_ref[0] = y


def _ffn_call(x, mod, norm_g, w_up, ffn_conv_w, ffn_conv_b, w_down, final_g, layer, *, ts, tf, rc, final):
    bsz, seq, d = x.shape
    n_layers, d_ff, _ = w_down.shape
    kw = ffn_conv_w.shape[1]
    nf = d_ff // tf
    cb = ffn_conv_b.reshape(n_layers, 1, 2 * d_ff)
    up = lambda f: jnp.minimum(f, nf - 1)
    dn = lambda f: jnp.maximum(f - 1, 0)
    kern = functools.partial(_ffn_kernel, ts=ts, rc=rc, nf=nf, conv_w=kw, final=final)
    return pl.pallas_call(
        kern,
        grid=(bsz, seq // ts, nf + 1),
        in_specs=[
            pl.BlockSpec((1, ts, d), lambda b, t, f: (b, t, 0), pipeline_mode=pl.Buffered(1)),
            pl.BlockSpec((1, N_ADA, d), lambda b, t, f: (b, 0, 0)),
            pl.BlockSpec((1, 1, d), lambda b, t, f: (layer, 0, 0)),
            pl.BlockSpec((1, 1, d, tf), lambda b, t, f: (layer, up(f), 0, 0)),
            pl.BlockSpec((1, 1, d, tf), lambda b, t, f: (layer, nf + up(f), 0, 0)),
            pl.BlockSpec((1, kw, tf), lambda b, t, f: (layer, 0, dn(f))),
            pl.BlockSpec((1, kw, tf), lambda b, t, f: (layer, 0, nf + dn(f))),
            pl.BlockSpec((1, 1, tf), lambda b, t, f: (layer, 0, dn(f))),
            pl.BlockSpec((1, 1, tf), lambda b, t, f: (layer, 0, nf + dn(f))),
            pl.BlockSpec((1, tf, d), lambda b, t, f: (layer, dn(f), 0)),
            pl.BlockSpec((1, d), lambda b, t, f: (0, 0)),
        ],
        out_specs=pl.BlockSpec((1, ts, d), lambda b, t, f: (b, t, 0)),
        out_shape=jax.ShapeDtypeStruct((bsz, seq, d), F32),
        scratch_shapes=[
            pltpu.VMEM((ts, d), BF16),
            pltpu.VMEM((ts, tf), F32),
            pltpu.VMEM((ts, tf), F32),
            pltpu.VMEM((nf, SUBLANES, tf), F32),
            pltpu.VMEM((nf, SUBLANES, tf), F32),
        ],
        compiler_params=pltpu.CompilerParams(
            dimension_semantics=("arbitrary", "arbitrary", "arbitrary"), vmem_limit_bytes=VMEM_LIMIT),
        name="conv_ffn",
    )(x, mod, norm_g.reshape(n_layers, 1, d), w_up, w_up, ffn_conv_w, ffn_conv_w, cb, cb,
      w_down, final_g.reshape(1, d))


def kernel(x, c, ada_w, ada_b, norm_mix_g, w_in, b_in, conv_w, conv_b, lru_wa, lru_ba, lru_wx, lru_bx, lru_lambda, pool_w, pool_b, pool_scale, proj_a, proj_b, w_out, norm_ffn_g, w_up, ffn_conv_w, ffn_conv_b, w_down, final_g):
    bsz, seq, d = x.shape
    n_layers = ada_w.shape[0]
    d_rnn = conv_w.shape[-1]
    d_pool = pool_b.shape[-1]
    assert lru_wa.shape[1] == LRU_HEADS and pool_w.shape[1] == len(POOL_WINDOWS)
    assert ada_w.shape[-1] == N_ADA * d

    c_pad = jnp.pad(c, ((0, SUBLANES - bsz), (0, 0)))
    mod_all = _ada_call(c_pad, ada_w, ada_b)[:, :bsz].reshape(n_layers, bsz, N_ADA, d)

    ts, tf = 1024, 512
    ts_wide, tn, bn = 2048, 512, 256
    w_up_b = _col_blocks(w_up, tf)
    w_down_b = _row_blocks(w_down, tf)
    for l in range(n_layers):
        mod = mod_all[l]
        zx, zg = _inproj_call(x, mod, norm_mix_g, w_in, b_in, l, d_rnn=d_rnn, d_pool=d_pool, ts=ts_wide, tn=tn, rc=256)
        ya = _rglru_call(zx, zg, conv_w, conv_b, lru_wa, lru_ba, lru_wx, lru_bx, lru_lambda, l,
                         d_rnn=d_rnn, ts=ts)
        yb = _pool_call(zx, pool_w, pool_b, pool_scale, l, d_rnn=d_rnn, d_pool=d_pool, ts=ts)
        x = _merge_call(ya, yb, zg, proj_a, proj_b, w_out, x, mod, l, ts=ts_wide, bn=bn)
        x = _ffn_call(x, mod, norm_ffn_g, w_up_b, ffn_conv_w, ffn_conv_b, w_down_b, final_g, l,
                      ts=ts, tf=tf, rc=256, final=(l == n_layers - 1))
    return x
```

```python
import functools

import jax
import jax.numpy as jnp
from jax import lax
from jax.experimental import pallas as pl
from jax.experimental.pallas import tpu as pltpu

F32 = jnp.float32
BF16 = jnp.bfloat16

LRU_HEADS = 8
LRU_C = 8.0
POOL_WINDOWS = (2, 4, 8, 16)
N_ADA = 6
EPS = 1e-6
GELU_C0 = 0.7978845608028654
GELU_C1 = 0.044715

SUBLANES = 8
POOL_HALO = 16
VMEM_LIMIT = 56 * 1024 * 1024


def _sigmoid(x):
    return 0.5 * (1.0 + jnp.tanh(0.5 * x))


def _gelu_tanh(x):
    return 0.5 * x * (1.0 + jnp.tanh(GELU_C0 * (x + GELU_C1 * (x * x * x))))


def _log_sigmoid(x):
    return jnp.minimum(x, 0.0) - jnp.log1p(jnp.exp(-jnp.abs(x)))


def _norm_mod(x, g, scale, shift):
    ms = jnp.mean(x * x, axis=-1, keepdims=True)
    y = (x * lax.rsqrt(ms + EPS)) * g
    return y * (1.0 + scale) + shift


def _cast_kernel(w_ref, o_ref):
    o_ref[...] = w_ref[...].astype(BF16).reshape(o_ref.shape)


def _col_blocks(w, tn):
    n_layers, k, n = w.shape
    return pl.pallas_call(
        _cast_kernel,
        grid=(n_layers, n // tn),
        in_specs=[pl.BlockSpec((1, k, tn), lambda l, j: (l, 0, j))],
        out_specs=pl.BlockSpec((1, 1, k, tn), lambda l, j: (l, j, 0, 0)),
        out_shape=jax.ShapeDtypeStruct((n_layers, n // tn, k, tn), BF16),
        compiler_params=pltpu.CompilerParams(dimension_semantics=("arbitrary", "arbitrary")),
        name="cast_cols",
    )(w)


def _shift_rows(v, k):
    return pltpu.roll(v, k, 0)


def _ada_kernel(c_ref, w_ref, b_ref, o_ref):
    c = c_ref[...]
    ca = (c * _sigmoid(c)).astype(BF16)
    o_ref[0] = jnp.dot(ca, w_ref[0].astype(BF16), preferred_element_type=F32) + b_ref[0]


def _ada_call(c_pad, ada_w, ada_b):
    n_layers, d, n_out = ada_w.shape
    tn = 2048
    rows = c_pad.shape[0]
    return pl.pallas_call(
        _ada_kernel,
        grid=(n_layers, n_out // tn),
        in_specs=[
            pl.BlockSpec((rows, d), lambda l, n: (0, 0)),
            pl.BlockSpec((1, d, tn), lambda l, n: (l, 0, n)),
            pl.BlockSpec((1, 1, tn), lambda l, n: (l, 0, n)),
        ],
        out_specs=pl.BlockSpec((1, rows, tn), lambda l, n: (l, 0, n)),
        out_shape=jax.ShapeDtypeStruct((n_layers, rows, n_out), F32),
        compiler_params=pltpu.CompilerParams(
            dimension_semantics=("arbitrary", "arbitrary"), vmem_limit_bytes=VMEM_LIMIT),
        name="ada_mod",
    )(c_pad, ada_w, ada_b.reshape(n_layers, 1, n_out))


def _inproj_kernel(x_ref, mod_ref, g_ref, w_ref, b_ref, zx_ref, zg_ref, h_ref, *, n_x, n_gelu, rc):
    n = pl.program_id(2)
    ts = h_ref.shape[0]
    chunks = [slice(r, r + rc) for r in range(0, ts, rc)]

    @pl.when(n == 0)
    def _():
        for rows in chunks:
            h_ref[rows, :] = _norm_mod(x_ref[0, rows, :], g_ref[0], mod_ref[0, 1:2, :],
                                       mod_ref[0, 0:1, :]).astype(BF16)

    def z_rows(w, rows):
        return jnp.dot(h_ref[rows, :], w, preferred_element_type=F32) + b_ref[0]


    @pl.when(n < n_x)
    def _():
        w = w_ref[0].astype(BF16)
        for rows in chunks:
            zx_ref[0, rows, :] = z_rows(w, rows)

    @pl.when(n >= n_x)
    def _():
        w = w_ref[0].astype(BF16)
        is_gelu = n < n_x + n_gelu
        for rows in chunks:
            z = z_rows(w, rows)
            arg = jnp.where(is_gelu, GELU_C0 * (z + GELU_C1 * (z * z * z)), 0.5 * z)
            lead = jnp.where(is_gelu, 0.5 * z, 0.5)
            zg_ref[0, rows, :] = (lead * (1.0 + jnp.tanh(arg))).astype(BF16)


def _inproj_call(x, mod, norm_g, w_in, b_in, layer, *, d_rnn, d_pool, ts, tn, rc):
    bsz, seq, d = x.shape
    n_layers, d_in = b_in.shape
    d_gates = d_in - 2 * d_rnn - d_pool
    nb_rnn, nb_pool, nb_gates = d_rnn // tn, d_pool // tn, d_gates // tn
    n_x = nb_rnn + nb_pool
    n_steps = n_x + nb_rnn + nb_gates

    def w_col(n):
        return jnp.where(n < nb_rnn, n,
                         jnp.where(n < n_x, n + nb_rnn,
                                   jnp.where(n < n_x + nb_rnn, n - nb_pool, n)))

    kern = functools.partial(_inproj_kernel, n_x=n_x, n_gelu=nb_rnn, rc=rc)
    return pl.pallas_call(
        kern,
        grid=(bsz, seq // ts, n_steps),
        in_specs=[
            pl.BlockSpec((1, ts, d), lambda b, t, n: (b, t, 0), pipeline_mode=pl.Buffered(1)),
            pl.BlockSpec((1, N_ADA, d), lambda b, t, n: (b, 0, 0)),
            pl.BlockSpec((1, 1, d), lambda b, t, n: (layer, 0, 0)),
            pl.BlockSpec((1, d, tn), lambda b, t, n: (layer, 0, w_col(n))),
            pl.BlockSpec((1, 1, tn), lambda b, t, n: (layer, 0, w_col(n))),
        ],
        out_specs=[
            pl.BlockSpec((1, ts, tn), lambda b, t, n: (b, t, jnp.minimum(n, n_x - 1))),
            pl.BlockSpec((1, ts, tn), lambda b, t, n: (b, t, jnp.maximum(n - n_x, 0))),
        ],
        out_shape=[
            jax.ShapeDtypeStruct((bsz, seq, d_rnn + d_pool), F32),
            jax.ShapeDtypeStruct((bsz, seq, d_rnn + d_gates), BF16),
        ],
        scratch_shapes=[pltpu.VMEM((ts, d), BF16)],
        compiler_params=pltpu.CompilerParams(
            dimension_semantics=("arbitrary", "arbitrary", "arbitrary"), vmem_limit_bytes=VMEM_LIMIT),
        name="in_proj",
    )(x, mod, norm_g.reshape(n_layers, 1, d), w_in, b_in.reshape(n_layers, 1, d_in))


def _rglru_kernel(xr_ref, gg_ref, cw_ref, cb_ref, wa_ref, ba_ref, wx_ref, bx_ref, lam_ref,
                  ya_ref, xbuf, a_buf, b_buf, h_buf, hcar, *, ts, conv_w):
    t = pl.program_id(2)
    halo = SUBLANES

    @pl.when(t == 0)
    def _():
        xbuf[0:halo, :] = jnp.zeros((halo, xbuf.shape[1]), F32)
        hcar[...] = jnp.zeros(hcar.shape, F32)

    xbuf[halo:halo + ts, :] = xr_ref[0]
    v = xbuf[...]
    cw = cw_ref[0]
    acc = cw[conv_w - 1:conv_w, :] * v
    for k in range(1, conv_w):
        acc = acc + cw[conv_w - 1 - k:conv_w - k, :] * _shift_rows(v, k)
    xr = acc[halo:, :] + cb_ref[0]
    xbuf[0:halo, :] = v[ts:ts + halo, :]

    xrb = xr.astype(BF16)
    r = _sigmoid(jnp.dot(xrb, wa_ref[0, 0].astype(BF16), preferred_element_type=F32) + ba_ref[0])
    i = _sigmoid(jnp.dot(xrb, wx_ref[0, 0].astype(BF16), preferred_element_type=F32) + bx_ref[0])
    log_a = (LRU_C * r) * _log_sigmoid(lam_ref[0])
    a = jnp.exp(log_a)
    b = jnp.sqrt(1.0 - a * a) * (i * xr)

    grouped = (ts // SUBLANES, SUBLANES, a.shape[-1])
    a = a.reshape(grouped)
    b = b.reshape(grouped)
    rowm = lax.broadcasted_iota(jnp.int32, grouped, 1)
    s = 1
    while s < SUBLANES:
        keep = rowm >= s
        a_sh = jnp.where(keep, pltpu.roll(a, s, 1), 1.0)
        b_sh = jnp.where(keep, pltpu.roll(b, s, 1), 0.0)
        b = a * b_sh + b
        a = a * a_sh
        s *= 2
    a_buf[...] = a.reshape(ts, grouped[-1])
    b_buf[...] = b.reshape(ts, grouped[-1])

    def body(g, hc):
        r0 = pl.multiple_of(g * SUBLANES, SUBLANES)
        h8 = a_buf[pl.ds(r0, SUBLANES), :] * hc + b_buf[pl.ds(r0, SUBLANES), :]
        h_buf[pl.ds(r0, SUBLANES), :] = h8
        return jnp.broadcast_to(h8[SUBLANES - 1:SUBLANES, :], hc.shape)

    hcar[...] = lax.fori_loop(0, ts // SUBLANES, body, hcar[...], unroll=8)
    ya_ref[0] = (h_buf[...] * gg_ref[0].astype(F32)).astype(BF16)


def _rglru_call(zx, zg, conv_w, conv_b, lru_wa, lru_ba, lru_wx, lru_bx, lru_lambda, layer, *, d_rnn, ts):
    bsz, seq, _ = zx.shape
    n_layers, kw, _ = conv_w.shape
    heads, wblk = lru_wa.shape[1], lru_wa.shape[2]
    vec = lambda a: a.reshape(n_layers, 1, d_rnn)
    vspec = pl.BlockSpec((1, 1, wblk), lambda b, j, t: (layer, 0, j))
    mspec = pl.BlockSpec((1, 1, wblk, wblk), lambda b, j, t: (layer, j, 0, 0))
    kern = functools.partial(_rglru_kernel, ts=ts, conv_w=kw)
    return pl.pallas_call(
        kern,
        grid=(bsz, heads, seq // ts),
        in_specs=[
            pl.BlockSpec((1, ts, wblk), lambda b, j, t: (b, t, j)),
            pl.BlockSpec((1, ts, wblk), lambda b, j, t: (b, t, j)),
            pl.BlockSpec((1, kw, wblk), lambda b, j, t: (layer, 0, j)),
            vspec, mspec, vspec, mspec, vspec, vspec,
        ],
        out_specs=pl.BlockSpec((1, ts, wblk), lambda b, j, t: (b, t, j)),
        out_shape=jax.ShapeDtypeStruct((bsz, seq, d_rnn), BF16),
        scratch_shapes=[
            pltpu.VMEM((SUBLANES + ts, wblk), F32),
            pltpu.VMEM((ts, wblk), F32),
            pltpu.VMEM((ts, wblk), F32),
            pltpu.VMEM((ts, wblk), F32),
            pltpu.VMEM((SUBLANES, wblk), F32),
        ],
        compiler_params=pltpu.CompilerParams(
            dimension_semantics=("arbitrary", "arbitrary", "arbitrary"), vmem_limit_bytes=VMEM_LIMIT),
        name="rg_lru",
    )(zx, zg, conv_w, vec(conv_b), lru_wa, vec(lru_ba), lru_wx, vec(lru_bx), vec(lru_lambda))


def _pool_kernel(xp_ref, w_ref, b_ref, s_ref, yb_ref, xbuf, *, ts, gw):
    t = pl.program_id(1)
    halo = POOL_HALO

    @pl.when(t == 0)
    def _():
        xbuf[0:halo, :] = jnp.zeros((halo, xbuf.shape[1]), F32)

    xbuf[halo:halo + ts, :] = xp_ref[0]
    pos = (t * ts + lax.broadcasted_iota(jnp.int32, (ts, gw), 0)).astype(F32)
    for g, win in enumerate(POOL_WINDOWS):
        v = xbuf[:, g * gw:(g + 1) * gw]
        sm = v
        k = 1
        while k < win:
            sm = sm + _shift_rows(sm, k)
            k *= 2
        cur = v[halo:, :]
        count = jnp.minimum(pos + 1.0, float(win))
        p = (sm[halo:, :] / count - cur).astype(BF16)
        y = jnp.dot(p, w_ref[0, g].astype(BF16), preferred_element_type=F32) + b_ref[0, :, g * gw:(g + 1) * gw]
        yb_ref[0, :, g * gw:(g + 1) * gw] = (y * s_ref[0, :, g * gw:(g + 1) * gw]).astype(BF16)
    xbuf[0:halo, :] = xbuf[ts:ts + halo, :]


def _pool_call(zx, pool_w, pool_b, pool_scale, layer, *, d_rnn, d_pool, ts):
    bsz, seq, _ = zx.shape
    n_layers, groups, gw, _ = pool_w.shape
    col0 = d_rnn // d_pool
    vec = lambda a: a.reshape(n_layers, 1, d_pool)
    vspec = pl.BlockSpec((1, 1, d_pool), lambda b, t: (layer, 0, 0))
    kern = functools.partial(_pool_kernel, ts=ts, gw=gw)
    return pl.pallas_call(
        kern,
        grid=(bsz, seq // ts),
        in_specs=[
            pl.BlockSpec((1, ts, d_pool), lambda b, t: (b, t, col0)),
            pl.BlockSpec((1, groups, gw, gw), lambda b, t: (layer, 0, 0, 0)),
            vspec, vspec,
        ],
        out_specs=pl.BlockSpec((1, ts, d_pool), lambda b, t: (b, t, 0)),
        out_shape=jax.ShapeDtypeStruct((bsz, seq, d_pool), BF16),
        scratch_shapes=[pltpu.VMEM((POOL_HALO + ts, d_pool), F32)],
        compiler_params=pltpu.CompilerParams(
            dimension_semantics=("arbitrary", "arbitrary"), vmem_limit_bytes=VMEM_LIMIT),
        name="ms_pool",
    )(zx, pool_w, vec(pool_b), vec(pool_scale))


def _merge_kernel(ya_ref, yb_ref, ga_ref, gb_ref, pa_ref, pb_ref, wo_ref, x_ref, mod_ref,
                  o_ref, m_ref, *, nb, bn):
    n = pl.program_id(2)

    @pl.when(n < nb)
    def _():
        pa = jnp.dot(ya_ref[0], pa_ref[0].astype(BF16), preferred_element_type=F32)
        pb = jnp.dot(yb_ref[0], pb_ref[0].astype(BF16), preferred_element_type=F32)
        merged = ga_ref[0].astype(F32) * pa + gb_ref[0].astype(F32) * pb
        c0 = pl.multiple_of(n * bn, bn)
        m_ref[:, pl.ds(c0, bn)] = merged.astype(BF16)

    @pl.when(n >= nb)
    def _():
        y = jnp.dot(m_ref[...], wo_ref[0].astype(BF16), preferred_element_type=F32)
        o_ref[0] = x_ref[0] + mod_ref[0, 2:3, :] * y


def _merge_call(ya, yb, zg, proj_a, proj_b, w_out, x, mod, layer, *, ts, bn):
    bsz, seq, d = x.shape
    d_rnn, d_pool = ya.shape[-1], yb.shape[-1]
    nb = d // bn
    ga0 = d_rnn // bn
    gb0 = ga0 + nb
    first = lambda n: jnp.minimum(n, nb - 1)
    second = lambda n: jnp.maximum(n - nb, 0)
    kern = functools.partial(_merge_kernel, nb=nb, bn=bn)
    return pl.pallas_call(
        kern,
        grid=(bsz, seq // ts, 2 * nb),
        in_specs=[
            pl.BlockSpec((1, ts, d_rnn), lambda b, t, n: (b, t, 0), pipeline_mode=pl.Buffered(1)),
            pl.BlockSpec((1, ts, d_pool), lambda b, t, n: (b, t, 0), pipeline_mode=pl.Buffered(1)),
            pl.BlockSpec((1, ts, bn), lambda b, t, n: (b, t, ga0 + first(n))),
            pl.BlockSpec((1, ts, bn), lambda b, t, n: (b, t, gb0 + first(n))),
            pl.BlockSpec((1, d_rnn, bn), lambda b, t, n: (layer, 0, first(n))),
            pl.BlockSpec((1, d_pool, bn), lambda b, t, n: (layer, 0, first(n))),
            pl.BlockSpec((1, d, bn), lambda b, t, n: (layer, 0, second(n))),
            pl.BlockSpec((1, ts, bn), lambda b, t, n: (b, t, second(n))),
            pl.BlockSpec((1, N_ADA, bn), lambda b, t, n: (b, 0, second(n))),
        ],
        out_specs=pl.BlockSpec((1, ts, bn), lambda b, t, n: (b, t, second(n))),
        out_shape=jax.ShapeDtypeStruct((bsz, seq, d), F32),
        scratch_shapes=[pltpu.VMEM((ts, d), BF16)],
        compiler_params=pltpu.CompilerParams(
            dimension_semantics=("arbitrary", "arbitrary", "arbitrary"), vmem_limit_bytes=VMEM_LIMIT),
        name="merge",
    )(ya, yb, zg, zg, proj_a, proj_b, w_out, x, mod)


def _ffn_kernel(x_ref, mod_ref, g_ref, wua_ref, wul_ref, cwa_ref, cwl_ref, cba_ref, cbl_ref,
                wd_ref, fg_ref, o_ref, h_ref, ua_buf, ul_buf, car_a, car_l, *, ts, rc, nf, conv_w, final):
    t = pl.program_id(1)
    f = pl.program_id(2)
    halo = SUBLANES
    n_chunks = ts // rc

    def conv(v, cw_ref, cb_ref):
        cw = cw_ref[0]
        acc = cw[conv_w - 1:conv_w, :] * v
        for k in range(1, conv_w):
            acc = acc + cw[conv_w - 1 - k:conv_w - k, :] * _shift_rows(v, k)
        return acc[halo:, :] + cb_ref[0]

    def run(do_up, do_down, first):
        if do_up:
            wa = wua_ref[0, 0]
            wl = wul_ref[0, 0]
        if do_down:
            wd = wd_ref[0].astype(BF16)
            tail_a = car_a[f - 1]
            tail_l = car_l[f - 1]
        for c in range(n_chunks):
            rows = slice(c * rc, (c + 1) * rc)
            if do_down:
                ua = ua_buf[rows, :]
                ul = ul_buf[rows, :]
                ca = conv(jnp.concatenate([tail_a, ua], axis=0), cwa_ref, cba_ref)
                cl = conv(jnp.concatenate([tail_l, ul], axis=0), cwl_ref, cbl_ref)
                p = ((ca * _sigmoid(ca)) * cl).astype(BF16)
                tail_a = ua[rc - halo:, :]
                tail_l = ul[rc - halo:, :]
            if do_up:
                h = h_ref[rows, :]
                ua_buf[rows, :] = jnp.dot(h, wa, preferred_element_type=F32)
                ul_buf[rows, :] = jnp.dot(h, wl, preferred_element_type=F32)
            if do_down:
                y = jnp.dot(p, wd, preferred_element_type=F32)
                if first:
                    o_ref[0, rows, :] = y
                else:
                    o_ref[0, rows, :] += y
        if do_down:
            car_a[f - 1] = tail_a
            car_l[f - 1] = tail_l

    row_chunks = [slice(c * rc, (c + 1) * rc) for c in range(n_chunks)]

    @pl.when(f == 0)
    def _():
        for rows in row_chunks:
            h_ref[rows, :] = _norm_mod(x_ref[0, rows, :], g_ref[0], mod_ref[0, 4:5, :],
                                       mod_ref[0, 3:4, :]).astype(BF16)
        run(True, False, False)

    @pl.when(jnp.logical_and(f > 0, t == 0))
    def _():
        car_a[f - 1] = jnp.zeros(car_a.shape[1:], F32)
        car_l[f - 1] = jnp.zeros(car_l.shape[1:], F32)

    @pl.when(f == 1)
    def _():
        run(True, True, True)

    @pl.when(jnp.logical_and(f > 1, f < nf))
    def _():
        run(True, True, False)

    @pl.when(f == nf)
    def _():
        run(False, True, False)
        for rows in row_chunks:
            y = x_ref[0, rows, :] + mod_ref[0, 5:6, :] * o_ref[0, rows, :]
            if final:
                ms = jnp.mean(y * y, axis=-1, keepdims=True)
                y = (y * lax.rsqrt(ms + EPS)) * fg_ref[...]
            o_ref[0, rows, :] = y


def _ffn_call(x, mod, norm_g, w_up, ffn_conv_w, ffn_conv_b, w_down, final_g, layer, *, ts, tf, rc, final):
    bsz, seq, d = x.shape
    n_layers, d_ff, _ = w_down.shape
    kw = ffn_conv_w.shape[1]
    nf = d_ff // tf
    cb = ffn_conv_b.reshape(n_layers, 1, 2 * d_ff)
    up = lambda f: jnp.minimum(f, nf - 1)
    dn = lambda f: jnp.maximum(f - 1, 0)
    kern = functools.partial(_ffn_kernel, ts=ts, rc=rc, nf=nf, conv_w=kw, final=final)
    return pl.pallas_call(
        kern,
        grid=(bsz, seq // ts, nf + 1),
        in_specs=[
            pl.BlockSpec((1, ts, d), lambda b, t, f: (b, t, 0), pipeline_mode=pl.Buffered(1)),
            pl.BlockSpec((1, N_ADA, d), lambda b, t, f: (b, 0, 0)),
            pl.BlockSpec((1, 1, d), lambda b, t, f: (layer, 0, 0)),
            pl.BlockSpec((1, 1, d, tf), lambda b, t, f: (layer, up(f), 0, 0)),
            pl.BlockSpec((1, 1, d, tf), lambda b, t, f: (layer, nf + up(f), 0, 0)),
            pl.BlockSpec((1, kw, tf), lambda b, t, f: (layer, 0, dn(f))),
            pl.BlockSpec((1, kw, tf), lambda b, t, f: (layer, 0, nf + dn(f))),
            pl.BlockSpec((1, 1, tf), lambda b, t, f: (layer, 0, dn(f))),
            pl.BlockSpec((1, 1, tf), lambda b, t, f: (layer, 0, nf + dn(f))),
            pl.BlockSpec((1, tf, d), lambda b, t, f: (layer, dn(f), 0)),
            pl.BlockSpec((1, d), lambda b, t, f: (0, 0)),
        ],
        out_specs=pl.BlockSpec((1, ts, d), lambda b, t, f: (b, t, 0)),
        out_shape=jax.ShapeDtypeStruct((bsz, seq, d), F32),
        scratch_shapes=[
            pltpu.VMEM((ts, d), BF16),
            pltpu.VMEM((ts, tf), F32),
            pltpu.VMEM((ts, tf), F32),
            pltpu.VMEM((nf, SUBLANES, tf), F32),
            pltpu.VMEM((nf, SUBLANES, tf), F32),
        ],
        compiler_params=pltpu.CompilerParams(
            dimension_semantics=("arbitrary", "arbitrary", "arbitrary"), vmem_limit_bytes=VMEM_LIMIT),
        name="conv_ffn",
    )(x, mod, norm_g.reshape(n_layers, 1, d), w_up, w_up, ffn_conv_w, ffn_conv_w, cb, cb,
      w_down, final_g.reshape(1, d))


def kernel(x, c, ada_w, ada_b, norm_mix_g, w_in, b_in, conv_w, conv_b, lru_wa, lru_ba, lru_wx, lru_bx, lru_lambda, pool_w, pool_b, pool_scale, proj_a, proj_b, w_out, norm_ffn_g, w_up, ffn_conv_w, ffn_conv_b, w_down, final_g):
    bsz, seq, d = x.shape
    n_layers = ada_w.shape[0]
    d_rnn = conv_w.shape[-1]
    d_pool = pool_b.shape[-1]
    assert lru_wa.shape[1] == LRU_HEADS and pool_w.shape[1] == len(POOL_WINDOWS)
    assert ada_w.shape[-1] == N_ADA * d

    c_pad = jnp.pad(c, ((0, SUBLANES - bsz), (0, 0)))
    mod_all = _ada_call(c_pad, ada_w, ada_b)[:, :bsz].reshape(n_layers, bsz, N_ADA, d)

    ts, tf = 1024, 512
    ts_wide, tn, bn = 2048, 512, 256
    w_up_b = _col_blocks(w_up, tf)
    for l in range(n_layers):
        mod = mod_all[l]
        zx, zg = _inproj_call(x, mod, norm_mix_g, w_in, b_in, l, d_rnn=d_rnn, d_pool=d_pool, ts=ts_wide, tn=tn, rc=256)
        ya = _rglru_call(zx, zg, conv_w, conv_b, lru_wa, lru_ba, lru_wx, lru_bx, lru_lambda, l,
                         d_rnn=d_rnn, ts=ts)
        yb = _pool_call(zx, pool_w, pool_b, pool_scale, l, d_rnn=d_rnn, d_pool=d_pool, ts=ts)
        x = _merge_call(ya, yb, zg, proj_a, proj_b, w_out, x, mod, l, ts=ts_wide, bn=bn)
        x = _ffn_call(x, mod, norm_ffn_g, w_up_b, ffn_conv_w, ffn_conv_b, w_down, final_g, l,
                      ts=ts, tf=tf, rc=256, final=(l == n_layers - 1))
    return x
```

```python
import functools

import jax
import jax.numpy as jnp
from jax import lax
from jax.experimental import pallas as pl
from jax.experimental.pallas import tpu as pltpu

F32 = jnp.float32
BF16 = jnp.bfloat16

LRU_HEADS = 8
LRU_C = 8.0
POOL_WINDOWS = (2, 4, 8, 16)
N_ADA = 6
EPS = 1e-6
GELU_C0 = 0.7978845608028654
GELU_C1 = 0.044715

SUBLANES = 8
POOL_HALO = 16
VMEM_LIMIT = 56 * 1024 * 1024


def _sigmoid(x):
    return 0.5 * (1.0 + jnp.tanh(0.5 * x))


def _gelu_tanh(x):
    return 0.5 * x * (1.0 + jnp.tanh(GELU_C0 * (x + GELU_C1 * (x * x * x))))


def _log_sigmoid(x):
    return jnp.minimum(x, 0.0) - jnp.log1p(jnp.exp(-jnp.abs(x)))


def _norm_mod(x, g, scale, shift):
    ms = jnp.mean(x * x, axis=-1, keepdims=True)
    y = (x * lax.rsqrt(ms + EPS)) * g
    return y * (1.0 + scale) + shift


def _cast_kernel(w_ref, o_ref):
    o_ref[...] = w_ref[...].astype(BF16).reshape(o_ref.shape)


def _col_blocks(w, tn):
    n_layers, k, n = w.shape
    return pl.pallas_call(
        _cast_kernel,
        grid=(n_layers, n // tn),
        in_specs=[pl.BlockSpec((1, k, tn), lambda l, j: (l, 0, j))],
        out_specs=pl.BlockSpec((1, 1, k, tn), lambda l, j: (l, j, 0, 0)),
        out_shape=jax.ShapeDtypeStruct((n_layers, n // tn, k, tn), BF16),
        compiler_params=pltpu.CompilerParams(dimension_semantics=("arbitrary", "arbitrary")),
        name="cast_cols",
    )(w)


def _shift_rows(v, k):
    return pltpu.roll(v, k, 0)


def _ada_kernel(c_ref, w_ref, b_ref, o_ref):
    c = c_ref[...]
    ca = (c * _sigmoid(c)).astype(BF16)
    o_ref[0] = jnp.dot(ca, w_ref[0].astype(BF16), preferred_element_type=F32) + b_ref[0]


def _ada_call(c_pad, ada_w, ada_b):
    n_layers, d, n_out = ada_w.shape
    tn = 2048
    rows = c_pad.shape[0]
    return pl.pallas_call(
        _ada_kernel,
        grid=(n_layers, n_out // tn),
        in_specs=[
            pl.BlockSpec((rows, d), lambda l, n: (0, 0)),
            pl.BlockSpec((1, d, tn), lambda l, n: (l, 0, n)),
            pl.BlockSpec((1, 1, tn), lambda l, n: (l, 0, n)),
        ],
        out_specs=pl.BlockSpec((1, rows, tn), lambda l, n: (l, 0, n)),
        out_shape=jax.ShapeDtypeStruct((n_layers, rows, n_out), F32),
        compiler_params=pltpu.CompilerParams(
            dimension_semantics=("arbitrary", "arbitrary"), vmem_limit_bytes=VMEM_LIMIT),
        name="ada_mod",
    )(c_pad, ada_w, ada_b.reshape(n_layers, 1, n_out))


def _inproj_kernel(x_ref, mod_ref, g_ref, w_ref, b_ref, zx_ref, zg_ref, h_ref, *, n_x, n_gelu, rc):
    n = pl.program_id(2)
    ts = h_ref.shape[0]
    chunks = [slice(r, r + rc) for r in range(0, ts, rc)]

    @pl.when(n == 0)
    def _():
        for rows in chunks:
            h_ref[rows, :] = _norm_mod(x_ref[0, rows, :], g_ref[0], mod_ref[0, 1:2, :],
                                       mod_ref[0, 0:1, :]).astype(BF16)

    def z_rows(w, rows):
        return jnp.dot(h_ref[rows, :], w, preferred_element_type=F32) + b_ref[0]


    @pl.when(n < n_x)
    def _():
        w = w_ref[0].astype(BF16)
        for rows in chunks:
            zx_ref[0, rows, :] = z_rows(w, rows)

    @pl.when(n >= n_x)
    def _():
        w = w_ref[0].astype(BF16)
        is_gelu = n < n_x + n_gelu
        for rows in chunks:
            z = z_rows(w, rows)
            arg = jnp.where(is_gelu, GELU_C0 * (z + GELU_C1 * (z * z * z)), 0.5 * z)
            lead = jnp.where(is_gelu, 0.5 * z, 0.5)
            zg_ref[0, rows, :] = (lead * (1.0 + jnp.tanh(arg))).astype(BF16)


def _inproj_call(x, mod, norm_g, w_in, b_in, layer, *, d_rnn, d_pool, ts, tn, rc):
    bsz, seq, d = x.shape
    n_layers, d_in = b_in.shape
    d_gates = d_in - 2 * d_rnn - d_pool
    nb_rnn, nb_pool, nb_gates = d_rnn // tn, d_pool // tn, d_gates // tn
    n_x = nb_rnn + nb_pool
    n_steps = n_x + nb_rnn + nb_gates

    def w_col(n):
        return jnp.where(n < nb_rnn, n,
                         jnp.where(n < n_x, n + nb_rnn,
                                   jnp.where(n < n_x + nb_rnn, n - nb_pool, n)))

    kern = functools.partial(_inproj_kernel, n_x=n_x, n_gelu=nb_rnn, rc=rc)
    return pl.pallas_call(
        kern,
        grid=(bsz, seq // ts, n_steps),
        in_specs=[
            pl.BlockSpec((1, ts, d), lambda b, t, n: (b, t, 0), pipeline_mode=pl.Buffered(1)),
            pl.BlockSpec((1, N_ADA, d), lambda b, t, n: (b, 0, 0)),
            pl.BlockSpec((1, 1, d), lambda b, t, n: (layer, 0, 0)),
            pl.BlockSpec((1, d, tn), lambda b, t, n: (layer, 0, w_col(n))),
            pl.BlockSpec((1, 1, tn), lambda b, t, n: (layer, 0, w_col(n))),
        ],
        out_specs=[
            pl.BlockSpec((1, ts, tn), lambda b, t, n: (b, t, jnp.minimum(n, n_x - 1))),
            pl.BlockSpec((1, ts, tn), lambda b, t, n: (b, t, jnp.maximum(n - n_x, 0))),
        ],
        out_shape=[
            jax.ShapeDtypeStruct((bsz, seq, d_rnn + d_pool), F32),
            jax.ShapeDtypeStruct((bsz, seq, d_rnn + d_gates), BF16),
        ],
        scratch_shapes=[pltpu.VMEM((ts, d), BF16)],
        compiler_params=pltpu.CompilerParams(
            dimension_semantics=("arbitrary", "arbitrary", "arbitrary"), vmem_limit_bytes=VMEM_LIMIT),
        name="in_proj",
    )(x, mod, norm_g.reshape(n_layers, 1, d), w_in, b_in.reshape(n_layers, 1, d_in))


def _rglru_kernel(xr_ref, gg_ref, cw_ref, cb_ref, wa_ref, ba_ref, wx_ref, bx_ref, lam_ref,
                  ya_ref, xbuf, a_buf, b_buf, h_buf, hcar, *, ts, conv_w):
    t = pl.program_id(2)
    halo = SUBLANES

    @pl.when(t == 0)
    def _():
        xbuf[0:halo, :] = jnp.zeros((halo, xbuf.shape[1]), F32)
        hcar[...] = jnp.zeros(hcar.shape, F32)

    xbuf[halo:halo + ts, :] = xr_ref[0]
    v = xbuf[...]
    cw = cw_ref[0]
    acc = cw[conv_w - 1:conv_w, :] * v
    for k in range(1, conv_w):
        acc = acc + cw[conv_w - 1 - k:conv_w - k, :] * _shift_rows(v, k)
    xr = acc[halo:, :] + cb_ref[0]
    xbuf[0:halo, :] = v[ts:ts + halo, :]

    xrb = xr.astype(BF16)
    r = _sigmoid(jnp.dot(xrb, wa_ref[0, 0].astype(BF16), preferred_element_type=F32) + ba_ref[0])
    i = _sigmoid(jnp.dot(xrb, wx_ref[0, 0].astype(BF16), preferred_element_type=F32) + bx_ref[0])
    log_a = (LRU_C * r) * _log_sigmoid(lam_ref[0])
    a = jnp.exp(log_a)
    b = jnp.sqrt(1.0 - a * a) * (i * xr)

    grouped = (ts // SUBLANES, SUBLANES, a.shape[-1])
    a = a.reshape(grouped)
    b = b.reshape(grouped)
    rowm = lax.broadcasted_iota(jnp.int32, grouped, 1)
    s = 1
    while s < SUBLANES:
        keep = rowm >= s
        a_sh = jnp.where(keep, pltpu.roll(a, s, 1), 1.0)
        b_sh = jnp.where(keep, pltpu.roll(b, s, 1), 0.0)
        b = a * b_sh + b
        a = a * a_sh
        s *= 2
    a_buf[...] = a.reshape(ts, grouped[-1])
    b_buf[...] = b.reshape(ts, grouped[-1])

    def body(g, hc):
        r0 = pl.multiple_of(g * SUBLANES, SUBLANES)
        h8 = a_buf[pl.ds(r0, SUBLANES), :] * hc + b_buf[pl.ds(r0, SUBLANES), :]
        h_buf[pl.ds(r0, SUBLANES), :] = h8
        return jnp.broadcast_to(h8[SUBLANES - 1:SUBLANES, :], hc.shape)

    hcar[...] = lax.fori_loop(0, ts // SUBLANES, body, hcar[...], unroll=8)
    ya_ref[0] = (h_buf[...] * gg_ref[0].astype(F32)).astype(BF16)


def _rglru_call(zx, zg, conv_w, conv_b, lru_wa, lru_ba, lru_wx, lru_bx, lru_lambda, layer, *, d_rnn, ts):
    bsz, seq, _ = zx.shape
    n_layers, kw, _ = conv_w.shape
    heads, wblk = lru_wa.shape[1], lru_wa.shape[2]
    vec = lambda a: a.reshape(n_layers, 1, d_rnn)
    vspec = pl.BlockSpec((1, 1, wblk), lambda b, j, t: (layer, 0, j))
    mspec = pl.BlockSpec((1, 1, wblk, wblk), lambda b, j, t: (layer, j, 0, 0))
    kern = functools.partial(_rglru_kernel, ts=ts, conv_w=kw)
    return pl.pallas_call(
        kern,
        grid=(bsz, heads, seq // ts),
        in_specs=[
            pl.BlockSpec((1, ts, wblk), lambda b, j, t: (b, t, j)),
            pl.BlockSpec((1, ts, wblk), lambda b, j, t: (b, t, j)),
            pl.BlockSpec((1, kw, wblk), lambda b, j, t: (layer, 0, j)),
            vspec, mspec, vspec, mspec, vspec, vspec,
        ],
        out_specs=pl.BlockSpec((1, ts, wblk), lambda b, j, t: (b, t, j)),
        out_shape=jax.ShapeDtypeStruct((bsz, seq, d_rnn), BF16),
        scratch_shapes=[
            pltpu.VMEM((SUBLANES + ts, wblk), F32),
            pltpu.VMEM((ts, wblk), F32),
            pltpu.VMEM((ts, wblk), F32),
            pltpu.VMEM((ts, wblk), F32),
            pltpu.VMEM((SUBLANES, wblk), F32),
        ],
        compiler_params=pltpu.CompilerParams(
            dimension_semantics=("arbitrary", "arbitrary", "arbitrary"), vmem_limit_bytes=VMEM_LIMIT),
        name="rg_lru",
    )(zx, zg, conv_w, vec(conv_b), lru_wa, vec(lru_ba), lru_wx, vec(lru_bx), vec(lru_lambda))


def _pool_kernel(xp_ref, w_ref, b_ref, s_ref, yb_ref, xbuf, *, ts, gw):
    t = pl.program_id(1)
    halo = POOL_HALO

    @pl.when(t == 0)
    def _():
        xbuf[0:halo, :] = jnp.zeros((halo, xbuf.shape[1]), F32)

    xbuf[halo:halo + ts, :] = xp_ref[0]
    pos = (t * ts + lax.broadcasted_iota(jnp.int32, (ts, gw), 0)).astype(F32)
    for g, win in enumerate(POOL_WINDOWS):
        v = xbuf[:, g * gw:(g + 1) * gw]
        sm = v
        k = 1
        while k < win:
            sm = sm + _shift_rows(sm, k)
            k *= 2
        cur = v[halo:, :]
        count = jnp.minimum(pos + 1.0, float(win))
        p = (sm[halo:, :] / count - cur).astype(BF16)
        y = jnp.dot(p, w_ref[0, g].astype(BF16), preferred_element_type=F32) + b_ref[0, :, g * gw:(g + 1) * gw]
        yb_ref[0, :, g * gw:(g + 1) * gw] = (y * s_ref[0, :, g * gw:(g + 1) * gw]).astype(BF16)
    xbuf[0:halo, :] = xbuf[ts:ts + halo, :]


def _pool_call(zx, pool_w, pool_b, pool_scale, layer, *, d_rnn, d_pool, ts):
    bsz, seq, _ = zx.shape
    n_layers, groups, gw, _ = pool_w.shape
    col0 = d_rnn // d_pool
    vec = lambda a: a.reshape(n_layers, 1, d_pool)
    vspec = pl.BlockSpec((1, 1, d_pool), lambda b, t: (layer, 0, 0))
    kern = functools.partial(_pool_kernel, ts=ts, gw=gw)
    return pl.pallas_call(
        kern,
        grid=(bsz, seq // ts),
        in_specs=[
            pl.BlockSpec((1, ts, d_pool), lambda b, t: (b, t, col0)),
            pl.BlockSpec((1, groups, gw, gw), lambda b, t: (layer, 0, 0, 0)),
            vspec, vspec,
        ],
        out_specs=pl.BlockSpec((1, ts, d_pool), lambda b, t: (b, t, 0)),
        out_shape=jax.ShapeDtypeStruct((bsz, seq, d_pool), BF16),
        scratch_shapes=[pltpu.VMEM((POOL_HALO + ts, d_pool), F32)],
        compiler_params=pltpu.CompilerParams(
            dimension_semantics=("arbitrary", "arbitrary"), vmem_limit_bytes=VMEM_LIMIT),
        name="ms_pool",
    )(zx, pool_w, vec(pool_b), vec(pool_scale))


def _merge_kernel(ya_ref, yb_ref, ga_ref, gb_ref, pa_ref, pb_ref, wo_ref, x_ref, mod_ref,
                  o_ref, m_ref, *, nb, bn):
    n = pl.program_id(2)

    @pl.when(n < nb)
    def _():
        pa = jnp.dot(ya_ref[0], pa_ref[0].astype(BF16), preferred_element_type=F32)
        pb = jnp.dot(yb_ref[0], pb_ref[0].astype(BF16), preferred_element_type=F32)
        merged = ga_ref[0].astype(F32) * pa + gb_ref[0].astype(F32) * pb
        c0 = pl.multiple_of(n * bn, bn)
        m_ref[:, pl.ds(c0, bn)] = merged.astype(BF16)

    @pl.when(n >= nb)
    def _():
        y = jnp.dot(m_ref[...], wo_ref[0].astype(BF16), preferred_element_type=F32)
        o_ref[0] = x_ref[0] + mod_ref[0, 2:3, :] * y


def _merge_call(ya, yb, zg, proj_a, proj_b, w_out, x, mod, layer, *, ts, bn):
    bsz, seq, d = x.shape
    d_rnn, d_pool = ya.shape[-1], yb.shape[-1]
    nb = d // bn
    ga0 = d_rnn // bn
    gb0 = ga0 + nb
    first = lambda n: jnp.minimum(n, nb - 1)
    second = lambda n: jnp.maximum(n - nb, 0)
    kern = functools.partial(_merge_kernel, nb=nb, bn=bn)
    return pl.pallas_call(
        kern,
        grid=(bsz, seq // ts, 2 * nb),
        in_specs=[
            pl.BlockSpec((1, ts, d_rnn), lambda b, t, n: (b, t, 0), pipeline_mode=pl.Buffered(1)),
            pl.BlockSpec((1, ts, d_pool), lambda b, t, n: (b, t, 0), pipeline_mode=pl.Buffered(1)),
            pl.BlockSpec((1, ts, bn), lambda b, t, n: (b, t, ga0 + first(n))),
            pl.BlockSpec((1, ts, bn), lambda b, t, n: (b, t, gb0 + first(n))),
            pl.BlockSpec((1, d_rnn, bn), lambda b, t, n: (layer, 0, first(n))),
            pl.BlockSpec((1, d_pool, bn), lambda b, t, n: (layer, 0, first(n))),
            pl.BlockSpec((1, d, bn), lambda b, t, n: (layer, 0, second(n))),
            pl.BlockSpec((1, ts, bn), lambda b, t, n: (b, t, second(n))),
            pl.BlockSpec((1, N_ADA, bn), lambda b, t, n: (b, 0, second(n))),
        ],
        out_specs=pl.BlockSpec((1, ts, bn), lambda b, t, n: (b, t, second(n))),
        out_shape=jax.ShapeDtypeStruct((bsz, seq, d), F32),
        scratch_shapes=[pltpu.VMEM((ts, d), BF16)],
        compiler_params=pltpu.CompilerParams(
            dimension_semantics=("arbitrary", "arbitrary", "arbitrary"), vmem_limit_bytes=VMEM_LIMIT),
        name="merge",
    )(ya, yb, zg, zg, proj_a, proj_b, w_out, x, mod)


def _ffn_kernel(x_ref, mod_ref, g_ref, wua_ref, wul_ref, cwa_ref, cwl_ref, cba_ref, cbl_ref,
                wd_ref, fg_ref, o_ref, h_ref, ua_buf, ul_buf, car_a, car_l, *, ts, rc, nf, conv_w, final):
    t = pl.program_id(1)
    f = pl.program_id(2)
    halo = SUBLANES
    n_chunks = ts // rc

    def conv(v, cw_ref, cb_ref):
        cw = cw_ref[0]
        acc = cw[conv_w - 1:conv_w, :] * v
        for k in range(1, conv_w):
            acc = acc + cw[conv_w - 1 - k:conv_w - k, :] * _shift_rows(v, k)
        return acc[halo:, :] + cb_ref[0]

    def run(do_up, do_down, first):
        if do_up:
            wa = wua_ref[0, 0]
            wl = wul_ref[0, 0]
        if do_down:
            wd = wd_ref[0].astype(BF16)
            tail_a = car_a[f - 1]
            tail_l = car_l[f - 1]
        for c in range(n_chunks):
            rows = slice(c * rc, (c + 1) * rc)
            if do_down:
                ua = ua_buf[rows, :]
                ul = ul_buf[rows, :]
                ca = conv(jnp.concatenate([tail_a, ua], axis=0), cwa_ref, cba_ref)
                cl = conv(jnp.concatenate([tail_l, ul], axis=0), cwl_ref, cbl_ref)
                p = ((ca * _sigmoid(ca)) * cl).astype(BF16)
                tail_a = ua[rc - halo:, :]
                tail_l = ul[rc - halo:, :]
            if do_up:
                h = h_ref[rows, :]
                ua_buf[rows, :] = jnp.dot(h, wa, preferred_element_type=F32)
                ul_buf[rows, :] = jnp.dot(h, wl, preferred_element_type=F32)
            if do_down:
                y = jnp.dot(p, wd, preferred_element_type=F32)
                if first:
                    o_ref[0, rows, :] = y
                else:
                    o_ref[0, rows, :] += y
        if do_down:
            car_a[f - 1] = tail_a
            car_l[f - 1] = tail_l

    row_chunks = [slice(c * rc, (c + 1) * rc) for c in range(n_chunks)]

    @pl.when(f == 0)
    def _():
        for rows in row_chunks:
            h_ref[rows, :] = _norm_mod(x_ref[0, rows, :], g_ref[0], mod_ref[0, 4:5, :],
                                       mod_ref[0, 3:4, :]).astype(BF16)
        run(True, False, False)

    @pl.when(jnp.logical_and(f > 0, t == 0))
    def _():
        car_a[f - 1] = jnp.zeros(car_a.shape[1:], F32)
        car_l[f - 1] = jnp.zeros(car_l.shape[1:], F32)

    @pl.when(f == 1)
    def _():
        run(True, True, True)

    @pl.when(jnp.logical_and(f > 1, f < nf))
    def _():
        run(True, True, False)

    @pl.when(f == nf)
    def _():
        run(False, True, False)
        for rows in row_chunks:
            y = x_ref[0, rows, :] + mod_ref[0, 5:6, :] * o_ref[0, rows, :]
            if final:
                ms = jnp.mean(y * y, axis=-1, keepdims=True)
                y = (y * lax.rsqrt(ms + EPS)) * fg_ref[...]
            o_ref[0, rows, :] = y


def _ffn_call(x, mod, norm_g, w_up, ffn_conv_w, ffn_conv_b, w_down, final_g, layer, *, ts, tf, rc, final):
    bsz, seq, d = x.shape
    n_layers, d_ff, _ = w_down.shape
    kw = ffn_conv_w.shape[1]
    nf = d_ff // tf
    cb = ffn_conv_b.reshape(n_layers, 1, 2 * d_ff)
    up = lambda f: jnp.minimum(f, nf - 1)
    dn = lambda f: jnp.maximum(f - 1, 0)
    kern = functools.partial(_ffn_kernel, ts=ts, rc=rc, nf=nf, conv_w=kw, final=final)
    return pl.pallas_call(
        kern,
        grid=(bsz, seq // ts, nf + 1),
        in_specs=[
            pl.BlockSpec((1, ts, d), lambda b, t, f: (b, t, 0), pipeline_mode=pl.Buffered(1)),
            pl.BlockSpec((1, N_ADA, d), lambda b, t, f: (b, 0, 0)),
            pl.BlockSpec((1, 1, d), lambda b, t, f: (layer, 0, 0)),
            pl.BlockSpec((1, 1, d, tf), lambda b, t, f: (layer, up(f), 0, 0)),
            pl.BlockSpec((1, 1, d, tf), lambda b, t, f: (layer, nf + up(f), 0, 0)),
            pl.BlockSpec((1, kw, tf), lambda b, t, f: (layer, 0, dn(f))),
            pl.BlockSpec((1, kw, tf), lambda b, t, f: (layer, 0, nf + dn(f))),
            pl.BlockSpec((1, 1, tf), lambda b, t, f: (layer, 0, dn(f))),
            pl.BlockSpec((1, 1, tf), lambda b, t, f: (layer, 0, nf + dn(f))),
            pl.BlockSpec((1, tf, d), lambda b, t, f: (layer, dn(f), 0)),
            pl.BlockSpec((1, d), lambda b, t, f: (0, 0)),
        ],
        out_specs=pl.BlockSpec((1, ts, d), lambda b, t, f: (b, t, 0)),
        out_shape=jax.ShapeDtypeStruct((bsz, seq, d), F32),
        scratch_shapes=[
            pltpu.VMEM((ts, d), BF16),
            pltpu.VMEM((ts, tf), F32),
            pltpu.VMEM((ts, tf), F32),
            pltpu.VMEM((nf, SUBLANES, tf), F32),
            pltpu.VMEM((nf, SUBLANES, tf), F32),
        ],
        compiler_params=pltpu.CompilerParams(
            dimension_semantics=("arbitrary", "arbitrary", "arbitrary"), vmem_limit_bytes=VMEM_LIMIT),
        name="conv_ffn",
    )(x, mod, norm_g.reshape(n_layers, 1, d), w_up, w_up, ffn_conv_w, ffn_conv_w, cb, cb,
      w_down, final_g.reshape(1, d))


def kernel(x, c, ada_w, ada_b, norm_mix_g, w_in, b_in, conv_w, conv_b, lru_wa, lru_ba, lru_wx, lru_bx, lru_lambda, pool_w, pool_b, pool_scale, proj_a, proj_b, w_out, norm_ffn_g, w_up, ffn_conv_w, ffn_conv_b, w_down, final_g):
    bsz, seq, d = x.shape
    n_layers = ada_w.shape[0]
    d_rnn = conv_w.shape[-1]
    d_pool = pool_b.shape[-1]
    assert lru_wa.shape[1] == LRU_HEADS and pool_w.shape[1] == len(POOL_WINDOWS)
    assert ada_w.shape[-1] == N_ADA * d

    c_pad = jnp.pad(c, ((0, SUBLANES - bsz), (0, 0)))
    mod_all = _ada_call(c_pad, ada_w, ada_b)[:, :bsz].reshape(n_layers, bsz, N_ADA, d)

    ts, tf = 1024, 512
    ts_wide, tn, bn = 2048, 512, 256
    w_up_b = _col_blocks(w_up, tf)
    for l in range(n_layers):
        mod = mod_all[l]
        zx, zg = _inproj_call(x, mod, norm_mix_g, w_in, b_in, l, d_rnn=d_rnn, d_pool=d_pool, ts=ts_wide, tn=tn, rc=256)
        ya = _rglru_call(zx, zg, conv_w, conv_b, lru_wa, lru_ba, lru_wx, lru_bx, lru_lambda, l,
                         d_rnn=d_rnn, ts=ts_wide)
        yb = _pool_call(zx, pool_w, pool_b, pool_scale, l, d_rnn=d_rnn, d_pool=d_pool, ts=ts)
        x = _merge_call(ya, yb, zg, proj_a, proj_b, w_out, x, mod, l, ts=ts_wide, bn=bn)
        x = _ffn_call(x, mod, norm_ffn_g, w_up_b, ffn_conv_w, ffn_conv_b, w_down, final_g, l,
                      ts=ts, tf=tf, rc=256, final=(l == n_layers - 1))
    return x
```

```python
import functools

import jax
import jax.numpy as jnp
from jax import lax
from jax.experimental import pallas as pl
from jax.experimental.pallas import tpu as pltpu

F32 = jnp.float32
BF16 = jnp.bfloat16

LRU_HEADS = 8
LRU_C = 8.0
POOL_WINDOWS = (2, 4, 8, 16)
N_ADA = 6
EPS = 1e-6
GELU_C0 = 0.7978845608028654
GELU_C1 = 0.044715

SUBLANES = 8
POOL_HALO = 16
VMEM_LIMIT = 56 * 1024 * 1024


def _sigmoid(x):
    return 0.5 * (1.0 + jnp.tanh(0.5 * x))


def _gelu_tanh(x):
    return 0.5 * x * (1.0 + jnp.tanh(GELU_C0 * (x + GELU_C1 * (x * x * x))))


def _log_sigmoid(x):
    return jnp.minimum(x, 0.0) - jnp.log1p(jnp.exp(-jnp.abs(x)))


def _norm_mod(x, g, scale, shift):
    ms = jnp.mean(x * x, axis=-1, keepdims=True)
    y = (x * lax.rsqrt(ms + EPS)) * g
    return y * (1.0 + scale) + shift


def _cast_kernel(w_ref, o_ref):
    o_ref[...] = w_ref[...].astype(BF16).reshape(o_ref.shape)


def _col_blocks(w, tn):
    n_layers, k, n = w.shape
    return pl.pallas_call(
        _cast_kernel,
        grid=(n_layers, n // tn),
        in_specs=[pl.BlockSpec((1, k, tn), lambda l, j: (l, 0, j))],
        out_specs=pl.BlockSpec((1, 1, k, tn), lambda l, j: (l, j, 0, 0)),
        out_shape=jax.ShapeDtypeStruct((n_layers, n // tn, k, tn), BF16),
        compiler_params=pltpu.CompilerParams(dimension_semantics=("arbitrary", "arbitrary")),
        name="cast_cols",
    )(w)


def _shift_rows(v, k):
    return pltpu.roll(v, k, 0)


def _ada_kernel(c_ref, w_ref, b_ref, o_ref):
    c = c_ref[...]
    ca = (c * _sigmoid(c)).astype(BF16)
    o_ref[0] = jnp.dot(ca, w_ref[0].astype(BF16), preferred_element_type=F32) + b_ref[0]


def _ada_call(c_pad, ada_w, ada_b):
    n_layers, d, n_out = ada_w.shape
    tn = 2048
    rows = c_pad.shape[0]
    return pl.pallas_call(
        _ada_kernel,
        grid=(n_layers, n_out // tn),
        in_specs=[
            pl.BlockSpec((rows, d), lambda l, n: (0, 0)),
            pl.BlockSpec((1, d, tn), lambda l, n: (l, 0, n)),
            pl.BlockSpec((1, 1, tn), lambda l, n: (l, 0, n)),
        ],
        out_specs=pl.BlockSpec((1, rows, tn), lambda l, n: (l, 0, n)),
        out_shape=jax.ShapeDtypeStruct((n_layers, rows, n_out), F32),
        compiler_params=pltpu.CompilerParams(
            dimension_semantics=("arbitrary", "arbitrary"), vmem_limit_bytes=VMEM_LIMIT),
        name="ada_mod",
    )(c_pad, ada_w, ada_b.reshape(n_layers, 1, n_out))


def _inproj_kernel(x_ref, mod_ref, g_ref, w_ref, b_ref, zx_ref, zg_ref, h_ref, *, n_x, n_gelu, rc):
    n = pl.program_id(2)
    ts = h_ref.shape[0]
    chunks = [slice(r, r + rc) for r in range(0, ts, rc)]

    @pl.when(n == 0)
    def _():
        for rows in chunks:
            h_ref[rows, :] = _norm_mod(x_ref[0, rows, :], g_ref[0], mod_ref[0, 1:2, :],
                                       mod_ref[0, 0:1, :]).astype(BF16)

    def z_rows(w, rows):
        return jnp.dot(h_ref[rows, :], w, preferred_element_type=F32) + b_ref[0]


    @pl.when(n < n_x)
    def _():
        w = w_ref[0].astype(BF16)
        for rows in chunks:
            zx_ref[0, rows, :] = z_rows(w, rows)

    @pl.when(n >= n_x)
    def _():
        w = w_ref[0].astype(BF16)
        is_gelu = n < n_x + n_gelu
        for rows in chunks:
            z = z_rows(w, rows)
            arg = jnp.where(is_gelu, GELU_C0 * (z + GELU_C1 * (z * z * z)), 0.5 * z)
            lead = jnp.where(is_gelu, 0.5 * z, 0.5)
            zg_ref[0, rows, :] = (lead * (1.0 + jnp.tanh(arg))).astype(BF16)


def _inproj_call(x, mod, norm_g, w_in, b_in, layer, *, d_rnn, d_pool, ts, tn, rc):
    bsz, seq, d = x.shape
    n_layers, d_in = b_in.shape
    d_gates = d_in - 2 * d_rnn - d_pool
    nb_rnn, nb_pool, nb_gates = d_rnn // tn, d_pool // tn, d_gates // tn
    n_x = nb_rnn + nb_pool
    n_steps = n_x + nb_rnn + nb_gates

    def w_col(n):
        return jnp.where(n < nb_rnn, n,
                         jnp.where(n < n_x, n + nb_rnn,
                                   jnp.where(n < n_x + nb_rnn, n - nb_pool, n)))

    kern = functools.partial(_inproj_kernel, n_x=n_x, n_gelu=nb_rnn, rc=rc)
    return pl.pallas_call(
        kern,
        grid=(bsz, seq // ts, n_steps),
        in_specs=[
            pl.BlockSpec((1, ts, d), lambda b, t, n: (b, t, 0), pipeline_mode=pl.Buffered(1)),
            pl.BlockSpec((1, N_ADA, d), lambda b, t, n: (b, 0, 0)),
            pl.BlockSpec((1, 1, d), lambda b, t, n: (layer, 0, 0)),
            pl.BlockSpec((1, d, tn), lambda b, t, n: (layer, 0, w_col(n))),
            pl.BlockSpec((1, 1, tn), lambda b, t, n: (layer, 0, w_col(n))),
        ],
        out_specs=[
            pl.BlockSpec((1, ts, tn), lambda b, t, n: (b, t, jnp.minimum(n, n_x - 1))),
            pl.BlockSpec((1, ts, tn), lambda b, t, n: (b, t, jnp.maximum(n - n_x, 0))),
        ],
        out_shape=[
            jax.ShapeDtypeStruct((bsz, seq, d_rnn + d_pool), F32),
            jax.ShapeDtypeStruct((bsz, seq, d_rnn + d_gates), BF16),
        ],
        scratch_shapes=[pltpu.VMEM((ts, d), BF16)],
        compiler_params=pltpu.CompilerParams(
            dimension_semantics=("arbitrary", "arbitrary", "arbitrary"), vmem_limit_bytes=VMEM_LIMIT),
        name="in_proj",
    )(x, mod, norm_g.reshape(n_layers, 1, d), w_in, b_in.reshape(n_layers, 1, d_in))


def _rglru_kernel(xr_ref, gg_ref, cw_ref, cb_ref, wa_ref, ba_ref, wx_ref, bx_ref, lam_ref,
                  ya_ref, xbuf, a_buf, b_buf, h_buf, hcar, *, ts, conv_w):
    t = pl.program_id(2)
    halo = SUBLANES

    @pl.when(t == 0)
    def _():
        xbuf[0:halo, :] = jnp.zeros((halo, xbuf.shape[1]), F32)
        hcar[...] = jnp.zeros(hcar.shape, F32)

    xbuf[halo:halo + ts, :] = xr_ref[0]
    v = xbuf[...]
    cw = cw_ref[0]
    acc = cw[conv_w - 1:conv_w, :] * v
    for k in range(1, conv_w):
        acc = acc + cw[conv_w - 1 - k:conv_w - k, :] * _shift_rows(v, k)
    xr = acc[halo:, :] + cb_ref[0]
    xbuf[0:halo, :] = v[ts:ts + halo, :]

    xrb = xr.astype(BF16)
    r = _sigmoid(jnp.dot(xrb, wa_ref[0, 0].astype(BF16), preferred_element_type=F32) + ba_ref[0])
    i = _sigmoid(jnp.dot(xrb, wx_ref[0, 0].astype(BF16), preferred_element_type=F32) + bx_ref[0])
    log_a = (LRU_C * r) * _log_sigmoid(lam_ref[0])
    a = jnp.exp(log_a)
    b = jnp.sqrt(1.0 - a * a) * (i * xr)

    grouped = (ts // SUBLANES, SUBLANES, a.shape[-1])
    a = a.reshape(grouped)
    b = b.reshape(grouped)
    rowm = lax.broadcasted_iota(jnp.int32, grouped, 1)
    s = 1
    while s < SUBLANES:
        keep = rowm >= s
        a_sh = jnp.where(keep, pltpu.roll(a, s, 1), 1.0)
        b_sh = jnp.where(keep, pltpu.roll(b, s, 1), 0.0)
        b = a * b_sh + b
        a = a * a_sh
        s *= 2
    a_buf[...] = a.reshape(ts, grouped[-1])
    b_buf[...] = b.reshape(ts, grouped[-1])

    def body(g, hc):
        r0 = pl.multiple_of(g * SUBLANES, SUBLANES)
        h8 = a_buf[pl.ds(r0, SUBLANES), :] * hc + b_buf[pl.ds(r0, SUBLANES), :]
        h_buf[pl.ds(r0, SUBLANES), :] = h8
        return jnp.broadcast_to(h8[SUBLANES - 1:SUBLANES, :], hc.shape)

    hcar[...] = lax.fori_loop(0, ts // SUBLANES, body, hcar[...], unroll=8)
    ya_ref[0] = (h_buf[...] * gg_ref[0].astype(F32)).astype(BF16)


def _rglru_call(zx, zg, conv_w, conv_b, lru_wa, lru_ba, lru_wx, lru_bx, lru_lambda, layer, *, d_rnn, ts):
    bsz, seq, _ = zx.shape
    n_layers, kw, _ = conv_w.shape
    heads, wblk = lru_wa.shape[1], lru_wa.shape[2]
    vec = lambda a: a.reshape(n_layers, 1, d_rnn)
    vspec = pl.BlockSpec((1, 1, wblk), lambda b, j, t: (layer, 0, j))
    mspec = pl.BlockSpec((1, 1, wblk, wblk), lambda b, j, t: (layer, j, 0, 0))
    kern = functools.partial(_rglru_kernel, ts=ts, conv_w=kw)
    return pl.pallas_call(
        kern,
        grid=(bsz, heads, seq // ts),
        in_specs=[
            pl.BlockSpec((1, ts, wblk), lambda b, j, t: (b, t, j)),
            pl.BlockSpec((1, ts, wblk), lambda b, j, t: (b, t, j)),
            pl.BlockSpec((1, kw, wblk), lambda b, j, t: (layer, 0, j)),
            vspec, mspec, vspec, mspec, vspec, vspec,
        ],
        out_specs=pl.BlockSpec((1, ts, wblk), lambda b, j, t: (b, t, j)),
        out_shape=jax.ShapeDtypeStruct((bsz, seq, d_rnn), BF16),
        scratch_shapes=[
            pltpu.VMEM((SUBLANES + ts, wblk), F32),
            pltpu.VMEM((ts, wblk), F32),
            pltpu.VMEM((ts, wblk), F32),
            pltpu.VMEM((ts, wblk), F32),
            pltpu.VMEM((SUBLANES, wblk), F32),
        ],
        compiler_params=pltpu.CompilerParams(
            dimension_semantics=("arbitrary", "arbitrary", "arbitrary"), vmem_limit_bytes=VMEM_LIMIT),
        name="rg_lru",
    )(zx, zg, conv_w, vec(conv_b), lru_wa, vec(lru_ba), lru_wx, vec(lru_bx), vec(lru_lambda))


def _pool_kernel(xp_ref, w_ref, b_ref, s_ref, yb_ref, xbuf, *, ts, gw):
    t = pl.program_id(1)
    halo = POOL_HALO

    @pl.when(t == 0)
    def _():
        xbuf[0:halo, :] = jnp.zeros((halo, xbuf.shape[1]), F32)

    xbuf[halo:halo + ts, :] = xp_ref[0]
    pos = (t * ts + lax.broadcasted_iota(jnp.int32, (ts, gw), 0)).astype(F32)
    for g, win in enumerate(POOL_WINDOWS):
        v = xbuf[:, g * gw:(g + 1) * gw]
        sm = v
        k = 1
        while k < win:
            sm = sm + _shift_rows(sm, k)
            k *= 2
        cur = v[halo:, :]
        count = jnp.minimum(pos + 1.0, float(win))
        p = (sm[halo:, :] / count - cur).astype(BF16)
        y = jnp.dot(p, w_ref[0, g].astype(BF16), preferred_element_type=F32) + b_ref[0, :, g * gw:(g + 1) * gw]
        yb_ref[0, :, g * gw:(g + 1) * gw] = (y * s_ref[0, :, g * gw:(g + 1) * gw]).astype(BF16)
    xbuf[0:halo, :] = xbuf[ts:ts + halo, :]


def _pool_call(zx, pool_w, pool_b, pool_scale, layer, *, d_rnn, d_pool, ts):
    bsz, seq, _ = zx.shape
    n_layers, groups, gw, _ = pool_w.shape
    col0 = d_rnn // d_pool
    vec = lambda a: a.reshape(n_layers, 1, d_pool)
    vspec = pl.BlockSpec((1, 1, d_pool), lambda b, t: (layer, 0, 0))
    kern = functools.partial(_pool_kernel, ts=ts, gw=gw)
    return pl.pallas_call(
        kern,
        grid=(bsz, seq // ts),
        in_specs=[
            pl.BlockSpec((1, ts, d_pool), lambda b, t: (b, t, col0)),
            pl.BlockSpec((1, groups, gw, gw), lambda b, t: (layer, 0, 0, 0)),
            vspec, vspec,
        ],
        out_specs=pl.BlockSpec((1, ts, d_pool), lambda b, t: (b, t, 0)),
        out_shape=jax.ShapeDtypeStruct((bsz, seq, d_pool), BF16),
        scratch_shapes=[pltpu.VMEM((POOL_HALO + ts, d_pool), F32)],
        compiler_params=pltpu.CompilerParams(
            dimension_semantics=("arbitrary", "arbitrary"), vmem_limit_bytes=VMEM_LIMIT),
        name="ms_pool",
    )(zx, pool_w, vec(pool_b), vec(pool_scale))


def _merge_kernel(ya_ref, yb_ref, ga_ref, gb_ref, pa_ref, pb_ref, wo_ref, x_ref, mod_ref,
                  o_ref, m_ref, *, nb, bn):
    n = pl.program_id(2)

    @pl.when(n < nb)
    def _():
        pa = jnp.dot(ya_ref[0], pa_ref[0].astype(BF16), preferred_element_type=F32)
        pb = jnp.dot(yb_ref[0], pb_ref[0].astype(BF16), preferred_element_type=F32)
        merged = ga_ref[0].astype(F32) * pa + gb_ref[0].astype(F32) * pb
        c0 = pl.multiple_of(n * bn, bn)
        m_ref[:, pl.ds(c0, bn)] = merged.astype(BF16)

    @pl.when(n >= nb)
    def _():
        y = jnp.dot(m_ref[...], wo_ref[0].astype(BF16), preferred_element_type=F32)
        o_ref[0] = x_ref[0] + mod_ref[0, 2:3, :] * y


def _merge_call(ya, yb, zg, proj_a, proj_b, w_out, x, mod, layer, *, ts, bn):
    bsz, seq, d = x.shape
    d_rnn, d_pool = ya.shape[-1], yb.shape[-1]
    nb = d // bn
    ga0 = d_rnn // bn
    gb0 = ga0 + nb
    first = lambda n: jnp.minimum(n, nb - 1)
    second = lambda n: jnp.maximum(n - nb, 0)
    kern = functools.partial(_merge_kernel, nb=nb, bn=bn)
    return pl.pallas_call(
        kern,
        grid=(bsz, seq // ts, 2 * nb),
        in_specs=[
            pl.BlockSpec((1, ts, d_rnn), lambda b, t, n: (b, t, 0), pipeline_mode=pl.Buffered(1)),
            pl.BlockSpec((1, ts, d_pool), lambda b, t, n: (b, t, 0), pipeline_mode=pl.Buffered(1)),
            pl.BlockSpec((1, ts, bn), lambda b, t, n: (b, t, ga0 + first(n))),
            pl.BlockSpec((1, ts, bn), lambda b, t, n: (b, t, gb0 + first(n))),
            pl.BlockSpec((1, d_rnn, bn), lambda b, t, n: (layer, 0, first(n))),
            pl.BlockSpec((1, d_pool, bn), lambda b, t, n: (layer, 0, first(n))),
            pl.BlockSpec((1, d, bn), lambda b, t, n: (layer, 0, second(n))),
            pl.BlockSpec((1, ts, bn), lambda b, t, n: (b, t, second(n))),
            pl.BlockSpec((1, N_ADA, bn), lambda b, t, n: (b, 0, second(n))),
        ],
        out_specs=pl.BlockSpec((1, ts, bn), lambda b, t, n: (b, t, second(n))),
        out_shape=jax.ShapeDtypeStruct((bsz, seq, d), F32),
        scratch_shapes=[pltpu.VMEM((ts, d), BF16)],
        compiler_params=pltpu.CompilerParams(
            dimension_semantics=("arbitrary", "arbitrary", "arbitrary"), vmem_limit_bytes=VMEM_LIMIT),
        name="merge",
    )(ya, yb, zg, zg, proj_a, proj_b, w_out, x, mod)


def _ffn_kernel(x_ref, mod_ref, g_ref, wua_ref, wul_ref, cwa_ref, cwl_ref, cba_ref, cbl_ref,
                wd_ref, fg_ref, o_ref, h_ref, ua_buf, ul_buf, car_a, car_l, *, ts, rc, nf, conv_w, final):
    t = pl.program_id(1)
    f = pl.program_id(2)
    halo = SUBLANES
    n_chunks = ts // rc

    def conv(v, cw_ref, cb_ref):
        cw = cw_ref[0]
        acc = cw[conv_w - 1:conv_w, :] * v
        for k in range(1, conv_w):
            acc = acc + cw[conv_w - 1 - k:conv_w - k, :] * _shift_rows(v, k)
        return acc[halo:, :] + cb_ref[0]

    def run(do_up, do_down, first):
        if do_up:
            wa = wua_ref[0, 0]
            wl = wul_ref[0, 0]
        if do_down:
            wd = wd_ref[0].astype(BF16)
            tail_a = car_a[f - 1]
            tail_l = car_l[f - 1]
        for c in range(n_chunks):
            rows = slice(c * rc, (c + 1) * rc)
            if do_down:
                ua = ua_buf[rows, :]
                ul = ul_buf[rows, :]
                ca = conv(jnp.concatenate([tail_a, ua], axis=0), cwa_ref, cba_ref)
                cl = conv(jnp.concatenate([tail_l, ul], axis=0), cwl_ref, cbl_ref)
                p = ((ca * _sigmoid(ca)) * cl).astype(BF16)
                tail_a = ua[rc - halo:, :]
                tail_l = ul[rc - halo:, :]
            if do_up:
                h = h_ref[rows, :]
                ua_buf[rows, :] = jnp.dot(h, wa, preferred_element_type=F32)
                ul_buf[rows, :] = jnp.dot(h, wl, preferred_element_type=F32)
            if do_down:
                y = jnp.dot(p, wd, preferred_element_type=F32)
                if first:
                    o_ref[0, rows, :] = y
                else:
                    o_ref[0, rows, :] += y
        if do_down:
            car_a[f - 1] = tail_a
            car_l[f - 1] = tail_l

    row_chunks = [slice(c * rc, (c + 1) * rc) for c in range(n_chunks)]

    @pl.when(f == 0)
    def _():
        for rows in row_chunks:
            h_ref[rows, :] = _norm_mod(x_ref[0, rows, :], g_ref[0], mod_ref[0, 4:5, :],
                                       mod_ref[0, 3:4, :]).astype(BF16)
        run(True, False, False)

    @pl.when(jnp.logical_and(f > 0, t == 0))
    def _():
        car_a[f - 1] = jnp.zeros(car_a.shape[1:], F32)
        car_l[f - 1] = jnp.zeros(car_l.shape[1:], F32)

    @pl.when(f == 1)
    def _():
        run(True, True, True)

    @pl.when(jnp.logical_and(f > 1, f < nf))
    def _():
        run(True, True, False)

    @pl.when(f == nf)
    def _():
        run(False, True, False)
        for rows in row_chunks:
            y = x_ref[0, rows, :] + mod_ref[0, 5:6, :] * o_ref[0, rows, :]
            if final:
                ms = jnp.mean(y * y, axis=-1, keepdims=True)
                y = (y * lax.rsqrt(ms + EPS)) * fg_ref[...]
            o_ref[0, rows, :] = y


def _ffn_call(x, mod, norm_g, w_up, ffn_conv_w, ffn_conv_b, w_down, final_g, layer, *, ts, tf, rc, final):
    bsz, seq, d = x.shape
    n_layers, d_ff, _ = w_down.shape
    kw = ffn_conv_w.shape[1]
    nf = d_ff // tf
    cb = ffn_conv_b.reshape(n_layers, 1, 2 * d_ff)
    up = lambda f: jnp.minimum(f, nf - 1)
    dn = lambda f: jnp.maximum(f - 1, 0)
    kern = functools.partial(_ffn_kernel, ts=ts, rc=rc, nf=nf, conv_w=kw, final=final)
    return pl.pallas_call(
        kern,
        grid=(bsz, seq // ts, nf + 1),
        in_specs=[
            pl.BlockSpec((1, ts, d), lambda b, t, f: (b, t, 0), pipeline_mode=pl.Buffered(1)),
            pl.BlockSpec((1, N_ADA, d), lambda b, t, f: (b, 0, 0)),
            pl.BlockSpec((1, 1, d), lambda b, t, f: (layer, 0, 0)),
            pl.BlockSpec((1, 1, d, tf), lambda b, t, f: (layer, up(f), 0, 0)),
            pl.BlockSpec((1, 1, d, tf), lambda b, t, f: (layer, nf + up(f), 0, 0)),
            pl.BlockSpec((1, kw, tf), lambda b, t, f: (layer, 0, dn(f))),
            pl.BlockSpec((1, kw, tf), lambda b, t, f: (layer, 0, nf + dn(f))),
            pl.BlockSpec((1, 1, tf), lambda b, t, f: (layer, 0, dn(f))),
            pl.BlockSpec((1, 1, tf), lambda b, t, f: (layer, 0, nf + dn(f))),
            pl.BlockSpec((1, tf, d), lambda b, t, f: (layer, dn(f), 0)),
            pl.BlockSpec((1, d), lambda b, t, f: (0, 0)),
        ],
        out_specs=pl.BlockSpec((1, ts, d), lambda b, t, f: (b, t, 0)),
        out_shape=jax.ShapeDtypeStruct((bsz, seq, d), F32),
        scratch_shapes=[
            pltpu.VMEM((ts, d), BF16),
            pltpu.VMEM((ts, tf), F32),
            pltpu.VMEM((ts, tf), F32),
            pltpu.VMEM((nf, SUBLANES, tf), F32),
            pltpu.VMEM((nf, SUBLANES, tf), F32),
        ],
        compiler_params=pltpu.CompilerParams(
            dimension_semantics=("arbitrary", "arbitrary", "arbitrary"), vmem_limit_bytes=VMEM_LIMIT),
        name="conv_ffn",
    )(x, mod, norm_g.reshape(n_layers, 1, d), w_up, w_up, ffn_conv_w, ffn_conv_w, cb, cb,
      w_down, final_g.reshape(1, d))


def kernel(x, c, ada_w, ada_b, norm_mix_g, w_in, b_in, conv_w, conv_b, lru_wa, lru_ba, lru_wx, lru_bx, lru_lambda, pool_w, pool_b, pool_scale, proj_a, proj_b, w_out, norm_ffn_g, w_up, ffn_conv_w, ffn_conv_b, w_down, final_g):
    bsz, seq, d = x.shape
    n_layers = ada_w.shape[0]
    d_rnn = conv_w.shape[-1]
    d_pool = pool_b.shape[-1]
    assert lru_wa.shape[1] == LRU_HEADS and pool_w.shape[1] == len(POOL_WINDOWS)
    assert ada_w.shape[-1] == N_ADA * d

    c_pad = jnp.pad(c, ((0, SUBLANES - bsz), (0, 0)))
    mod_all = _ada_call(c_pad, ada_w, ada_b)[:, :bsz].reshape(n_layers, bsz, N_ADA, d)

    ts, tf = 1024, 512
    ts_wide, tn, bn = 2048, 512, 256
    w_up_b = _col_blocks(w_up, tf)
    for l in range(n_layers):
        mod = mod_all[l]
        zx, zg = _inproj_call(x, mod, norm_mix_g, w_in, b_in, l, d_rnn=d_rnn, d_pool=d_pool, ts=ts_wide, tn=tn, rc=256)
        ya = _rglru_call(zx, zg, conv_w, conv_b, lru_wa, lru_ba, lru_wx, lru_bx, lru_lambda, l,
                         d_rnn=d_rnn, ts=ts_wide)
        yb = _pool_call(zx, pool_w, pool_b, pool_scale, l, d_rnn=d_rnn, d_pool=d_pool, ts=ts)
        x = _merge_call(ya, yb, zg, proj_a, proj_b, w_out, x, mod, l, ts=ts_wide, bn=bn)
        x = _ffn_call(x, mod, norm_ffn_g, w_up_b, ffn_conv_w, ffn_conv_b, w_down, final_g, l,
                      ts=ts, tf=tf, rc=512, final=(l == n_layers - 1))
    return x
```
